```python
import math
import jax, jax.numpy as jnp
from jax import lax
import numpy as np

D_MODEL = 4096
BATCH = 4
SEQ = 2048
DEPTH = 1
DEC_BATCH = 128
DEC_SEQ = 1
PAST_LEN = 16384
PAGE_SIZE = 128

RET_HEADS = 8
RET_DIM = 256
ML_HEADS = 8
ML_DIM = 256
RET_W = RET_HEADS * RET_DIM
ML_W = ML_HEADS * ML_DIM
MIX_W = RET_W + ML_W
IN_COLS = 4 * RET_W + 4 * ML_W + 2 * ML_HEADS
CHUNK = 128
CONV_W = 4
ROPE_BASE = 10000.0
PEER_HEADS = 8
N_KEYS = 128
N_EXPERTS = N_KEYS * N_KEYS
PEER_TOPK = 16
PEER_QDIM = 256
PEER_BLOCK = 64
PLE_DIM = 256
LN_EPS = 1e-5
DEEPNORM_ALPHA = (2.0 * DEPTH) ** 0.25
DEEPNORM_BETA = (8.0 * DEPTH) ** -0.25

kernel_name = "hybrid_retention_mlstm_peer_decoder_step"

F32 = jnp.float32


def layer_norm(x, g, b):
    xf = x.astype(F32)
    mu = xf.mean(-1, keepdims=True)
    var = jnp.mean(jnp.square(xf - mu), -1, keepdims=True)
    return ((xf - mu) * lax.rsqrt(var + LN_EPS)).astype(x.dtype) * g + b


def head_norm(h, g):
    mu = h.mean(-1, keepdims=True)
    var = jnp.mean(jnp.square(h - mu), -1, keepdims=True)
    return (h - mu) * lax.rsqrt(var + LN_EPS) * g.astype(F32)


def rope(x, pos):
    half = x.shape[-1] // 2
    inv = ROPE_BASE ** (-jnp.arange(half, dtype=F32) / half)
    ang = pos.astype(F32)[:, None] * inv[None]
    cos = jnp.cos(ang)[None, :, None, :].astype(x.dtype)
    sin = jnp.sin(ang)[None, :, None, :].astype(x.dtype)
    x1, x2 = x[..., :half], x[..., half:]
    return jnp.concatenate([x1 * cos - x2 * sin, x1 * sin + x2 * cos], -1)


def to_chunks(a, chunk):
    B, T = a.shape[:2]
    return jnp.moveaxis(a.reshape(B, T // chunk, chunk, *a.shape[2:]), 1, 0)


def from_chunks(a):
    a = jnp.moveaxis(a, 0, 1)
    return a.reshape(a.shape[0], a.shape[1] * a.shape[2], *a.shape[3:])


def retention(q, k, v, s0, chunk):
    H = q.shape[2]
    log_g = jnp.log1p(-(2.0 ** (-5.0 - jnp.arange(H, dtype=F32))))
    idx = jnp.arange(chunk, dtype=F32)
    diff = idx[:, None] - idx[None, :]
    causal = diff >= 0
    decay_in = jnp.where(causal[None], jnp.exp(log_g[:, None, None] * jnp.where(causal, diff, 0.0)[None]), 0.0)
    decay_q = jnp.exp(log_g[None, :] * (idx + 1.0)[:, None])[None, :, :, None]
    decay_k = jnp.exp(log_g[None, :] * (chunk - 1.0 - idx)[:, None])[None, :, :, None]
    decay_c = jnp.exp(log_g * chunk)[None, :, None, None]

    def step(s, inp):
        qc, kc, vc = inp
        sc = jnp.einsum('bihd,bjhd->bhij', qc, kc) * decay_in
        o = jnp.einsum('bhij,bjhe->bihe', sc, vc) + jnp.einsum('bihd,bhde->bihe', qc, s) * decay_q
        s = s * decay_c + jnp.einsum('bjhd,bjhe->bhde', kc * decay_k, vc)
        return s, o

    s, o = lax.scan(step, s0, (to_chunks(q, chunk), to_chunks(k, chunk), to_chunks(v, chunk)))
    return from_chunks(o), s


def mlstm(q, k, v, ig, lf, c0, n0, m0, chunk):
    idx = jnp.arange(chunk)
    causal = idx[:, None] >= idx[None, :]

    def step(carry, inp):
        c, n, m = carry
        qc, kc, vc, ic, fc = inp
        bt = jnp.cumsum(fc, axis=1).transpose(0, 2, 1)
        it = ic.transpose(0, 2, 1)
        dmat = jnp.where(causal, bt[..., :, None] - bt[..., None, :] + it[..., None, :], -jnp.inf)
        prior = bt + m[..., None]
        mt = jnp.maximum(prior, dmat.max(-1))
        w = jnp.exp(dmat - mt[..., None])
        wp = jnp.exp(prior - mt).transpose(0, 2, 1)
        qk = jnp.einsum('bihd,bjhd->bhij', qc, kc) * w
        num = jnp.einsum('bhij,bjhe->bihe', qk, vc) + jnp.einsum('bihd,bhde->bihe', qc, c) * wp[..., None]
        den = qk.sum(-1).transpose(0, 2, 1) + jnp.einsum('bihd,bhd->bih', qc, n) * wp
        h = num / jnp.maximum(jnp.abs(den), jnp.exp(-mt).transpose(0, 2, 1))[..., None]
        bl = bt[..., -1]
        m_new = mt[..., -1]
        wk = jnp.exp(bl[..., None] - bt + it - m_new[..., None])
        wc = jnp.exp(bl + m - m_new)
        c_new = c * wc[..., None, None] + jnp.einsum('bjhd,bhj,bjhe->bhde', kc, wk, vc)
        n_new = n * wc[..., None] + jnp.einsum('bjhd,bhj->bhd', kc, wk)
        return (c_new, n_new, m_new), h

    xs = tuple(to_chunks(a, chunk) for a in (q, k, v, ig, lf))
    (c, n, m), h = lax.scan(step, (c0, n0, m0), xs)
    return from_chunks(h), c, n, m


def causal_conv(x, buf, w, b):
    T = x.shape[1]
    xx = jnp.concatenate([buf.astype(x.dtype), x], 1)
    y = sum(xx[:, j:j + T] * w[j] for j in range(CONV_W)) + b
    return y, xx[:, -(CONV_W - 1):]


def token_mixers(x, pos, chunk, s_ret, conv_buf, c0, n0, m0, w_in, b_gate, conv_w, conv_b, g_ret_norm, g_ml_norm, w_out):
    B, T, _ = x.shape
    z = x @ w_in
    offs = [RET_W, 2 * RET_W, 3 * RET_W, 4 * RET_W, 4 * RET_W + 2 * ML_W, 4 * RET_W + 3 * ML_W, 4 * RET_W + 4 * ML_W]
    rq, rk, rv, rg, mqk, mv, mo, gates = jnp.split(z, offs, axis=-1)
    heads = lambda a, d: a.reshape(B, T, -1, d)
    rq = rope(heads(rq, RET_DIM), pos).astype(F32)
    rk = (rope(heads(rk, RET_DIM), pos) * RET_DIM ** -0.5).astype(F32)
    o_r, s_new = retention(rq, rk, heads(rv, RET_DIM).astype(F32), s_ret.astype(F32), chunk)
    o_r = head_norm(o_r, g_ret_norm).reshape(B, T, RET_W).astype(x.dtype) * jax.nn.silu(rg)
    qk_c, buf_new = causal_conv(mqk, conv_buf, conv_w, conv_b)
    mq, mk = jnp.split(jax.nn.silu(qk_c), 2, axis=-1)
    g = gates.astype(F32) + b_gate.astype(F32)
    ig, lf = g[..., :ML_HEADS], jax.nn.log_sigmoid(g[..., ML_HEADS:])
    h, c, n, m = mlstm(heads(mq, ML_DIM).astype(F32), (heads(mk, ML_DIM) * ML_DIM ** -0.5).astype(F32),
                       heads(mv, ML_DIM).astype(F32), ig, lf, c0.astype(F32), n0.astype(F32), m0.astype(F32), chunk)
    o_m = head_norm(h, g_ml_norm).reshape(B, T, ML_W).astype(x.dtype) * jax.nn.sigmoid(mo)
    y = jnp.concatenate([o_r, o_m], -1) @ w_out
    return y, s_new, buf_new, c, n, m


def peer(x, w_q, sub_keys, u_tab, v_tab):
    lead = x.shape[:-1]
    xt = x.reshape(-1, D_MODEL)
    nt = xt.shape[0]
    q = (xt @ w_q).reshape(nt, PEER_HEADS, 2, PEER_QDIM // 2).astype(F32)
    s = jnp.einsum('nhpd,hpkd->nhpk', q, sub_keys.astype(F32))
    sv, si = lax.top_k(s, PEER_TOPK)
    cand = (sv[:, :, 0, :, None] + sv[:, :, 1, None, :]).reshape(nt, PEER_HEADS, -1)
    cidx = (si[:, :, 0, :, None] * N_KEYS + si[:, :, 1, None, :]).reshape(nt, PEER_HEADS, -1)
    tv, ti = lax.top_k(cand, PEER_TOPK)
    eidx = jnp.take_along_axis(cidx, ti, -1).reshape(nt, -1)
    gw = jax.nn.softmax(tv, -1).reshape(nt, -1)
    nblk = -(-nt // PEER_BLOCK)
    pad = nblk * PEER_BLOCK - nt
    xb = jnp.pad(xt, ((0, pad), (0, 0))).reshape(nblk, PEER_BLOCK, D_MODEL)
    eb = jnp.pad(eidx, ((0, pad), (0, 0))).reshape(nblk, PEER_BLOCK, -1)
    gb = jnp.pad(gw, ((0, pad), (0, 0))).reshape(nblk, PEER_BLOCK, -1)

    def block(args):
        xs, es, gs = args
        a = jax.nn.gelu(jnp.einsum('nd,nkd->nk', xs, u_tab[es]))
        return jnp.einsum('nk,nkd->nd', a * gs.astype(a.dtype), v_tab[es])

    y = lax.map(block, (xb, eb, gb)).reshape(-1, D_MODEL)[:nt]
    return y.reshape(*lead, D_MODEL)


def trunk(x, p, pos, chunk, s_ret, s_conv, s_c, s_n, s_m, ln_emb_g, ln_emb_b, w_in, b_gate, conv_w, conv_b,
          g_ret_norm, g_ml_norm, w_out, ln1_g, ln1_b, w_peer_q, peer_sub_keys, peer_u, peer_v,
          w_ple_gate, w_ple_proj, ln2_g, ln2_b):
    x = layer_norm(x, ln_emb_g, ln_emb_b)
    rets, convs, cs, ns, ms = [], [], [], [], []
    for i in range(DEPTH):
        mix, sr, cb, c, n, m = token_mixers(x, pos, chunk, s_ret[i], s_conv[i], s_c[i], s_n[i], s_m[i], w_in[i], b_gate[i],
                                            conv_w[i], conv_b[i], g_ret_norm[i], g_ml_norm[i], w_out[i])
        x = layer_norm(DEEPNORM_ALPHA * x + mix, ln1_g[i], ln1_b[i])
        ch = peer(x, w_peer_q[i], peer_sub_keys[i], peer_u[i], peer_v[i])
        ple = jax.nn.sigmoid(x @ w_ple_gate[i]) * (p[i].astype(x.dtype) @ w_ple_proj[i])
        x = layer_norm(DEEPNORM_ALPHA * x + ch + ple, ln2_g[i], ln2_b[i])
        rets.append(sr); convs.append(cb); cs.append(c); ns.append(n); ms.append(m)
    return x, jnp.stack(rets), jnp.stack(convs), jnp.stack(cs), jnp.stack(ns), jnp.stack(ms)


def setup_inputs(seed: int = 0) -> dict:
    key = jax.random.key(seed)
    ks = jax.random.split(key, 32)
    nrm = lambda k, shape, scale: jax.random.normal(k, shape, F32) * scale
    H = ML_HEADS
    b_gate = jnp.concatenate([nrm(ks[10], (DEPTH, H), 0.1),
                              jnp.broadcast_to(jnp.linspace(3.0, 6.0, H, dtype=F32), (DEPTH, H)) + nrm(ks[11], (DEPTH, H), 0.1)], -1)
    return {
        "x_prompt": nrm(ks[0], (BATCH, SEQ, D_MODEL), 1.0),
        "x_sample": nrm(ks[1], (DEC_BATCH, DEC_SEQ, D_MODEL), 1.0),
        "state_ret": nrm(ks[2], (DEPTH, DEC_BATCH, RET_HEADS, RET_DIM, RET_DIM), 0.1),
        "state_conv": nrm(ks[3], (DEPTH, DEC_BATCH, CONV_W - 1, 2 * ML_W), 1.0),
        "state_mlstm_c": nrm(ks[4], (DEPTH, DEC_BATCH, ML_HEADS, ML_DIM, ML_DIM), 0.1),
        "state_mlstm_n": nrm(ks[5], (DEPTH, DEC_BATCH, ML_HEADS, ML_DIM), 0.5),
        "state_mlstm_m": nrm(ks[6], (DEPTH, DEC_BATCH, ML_HEADS), 1.0),
        "p_prompt": nrm(ks[7], (DEPTH, BATCH, SEQ, PLE_DIM), 1.0),
        "p_sample": nrm(ks[8], (DEPTH, DEC_BATCH, DEC_SEQ, PLE_DIM), 1.0),
        "ln_emb_g": 1.0 + nrm(ks[9], (D_MODEL,), 0.02),
        "ln_emb_b": nrm(ks[12], (D_MODEL,), 0.02),
        "w_in": nrm(ks[13], (DEPTH, D_MODEL, IN_COLS), D_MODEL ** -0.5),
        "b_gate": b_gate,
        "conv_w": nrm(ks[14], (DEPTH, CONV_W, 2 * ML_W), CONV_W ** -0.5),
        "conv_b": nrm(ks[15], (DEPTH, 2 * ML_W), 0.02),
        "g_ret_norm": 1.0 + nrm(ks[16], (DEPTH, RET_HEADS, RET_DIM), 0.02),
        "g_ml_norm": 1.0 + nrm(ks[17], (DEPTH, ML_HEADS, ML_DIM), 0.02),
        "w_out": nrm(ks[18], (DEPTH, MIX_W, D_MODEL), MIX_W ** -0.5 * DEEPNORM_BETA),
        "ln1_g": 1.0 + nrm(ks[19], (DEPTH, D_MODEL), 0.02),
        "ln1_b": nrm(ks[20], (DEPTH, D_MODEL), 0.02),
        "w_peer_q": nrm(ks[21], (DEPTH, D_MODEL, PEER_HEADS * PEER_QDIM), D_MODEL ** -0.5),
        "peer_sub_keys": nrm(ks[22], (DEPTH, PEER_HEADS, 2, N_KEYS, PEER_QDIM // 2), (PEER_QDIM // 2) ** -0.5),
        "peer_u": nrm(ks[23], (DEPTH, N_EXPERTS, D_MODEL), D_MODEL ** -0.5),
        "peer_v": nrm(ks[24], (DEPTH, N_EXPERTS, D_MODEL), DEEPNORM_BETA * PEER_HEADS ** -0.5),
        "w_ple_gate": nrm(ks[25], (DEPTH, D_MODEL, D_MODEL), D_MODEL ** -0.5),
        "w_ple_proj": nrm(ks[26], (DEPTH, PLE_DIM, D_MODEL), PLE_DIM ** -0.5 * DEEPNORM_BETA),
        "ln2_g": 1.0 + nrm(ks[27], (DEPTH, D_MODEL), 0.02),
        "ln2_b": nrm(ks[28], (DEPTH, D_MODEL), 0.02),
    }


def reference(x_prompt, x_sample, state_ret, state_conv, state_mlstm_c, state_mlstm_n, state_mlstm_m, p_prompt, p_sample,
              ln_emb_g, ln_emb_b, w_in, b_gate, conv_w, conv_b, g_ret_norm, g_ml_norm, w_out, ln1_g, ln1_b,
              w_peer_q, peer_sub_keys, peer_u, peer_v, w_ple_gate, w_ple_proj, ln2_g, ln2_b):
    weights = (ln_emb_g, ln_emb_b, w_in, b_gate, conv_w, conv_b, g_ret_norm, g_ml_norm, w_out, ln1_g, ln1_b,
               w_peer_q, peer_sub_keys, peer_u, peer_v, w_ple_gate, w_ple_proj, ln2_g, ln2_b)
    Bp, Tp = x_prompt.shape[:2]
    y_prompt, ret_p, conv_p, c_p, n_p, m_p = trunk(
        x_prompt, p_prompt, jnp.arange(Tp), CHUNK,
        jnp.zeros((DEPTH, Bp, RET_HEADS, RET_DIM, RET_DIM), F32),
        jnp.zeros((DEPTH, Bp, CONV_W - 1, 2 * ML_W), x_prompt.dtype),
        jnp.zeros((DEPTH, Bp, ML_HEADS, ML_DIM, ML_DIM), F32),
        jnp.zeros((DEPTH, Bp, ML_HEADS, ML_DIM), F32),
        jnp.zeros((DEPTH, Bp, ML_HEADS), F32),
        *weights)
    Ts = x_sample.shape[1]
    y_sample, ret_s, conv_s, c_s, n_s, m_s = trunk(
        x_sample, p_sample, PAST_LEN + jnp.arange(Ts), Ts,
        state_ret, state_conv, state_mlstm_c, state_mlstm_n, state_mlstm_m,
        *weights)
    return (y_prompt, y_sample, ret_p, conv_p, c_p, n_p, m_p, ret_s, conv_s, c_s, n_s, m_s)
```

```python
import functools
import math

import jax
import jax.numpy as jnp
from jax import lax
from jax.experimental import pallas as pl
from jax.experimental.pallas import tpu as pltpu

F32 = jnp.float32
BF16 = jnp.bfloat16
I32 = jnp.int32

LN_EPS = 1e-5
DEPTH = 1
DEEPNORM_ALPHA = (2.0 * DEPTH) ** 0.25
CHUNK = 128
ROPE_BASE = 10000.0
PAST_LEN = 16384
PEER_TOPK = 16
CONV_W = 4

V7X_VMEM_LIMIT_BYTES = 56 * 1024 * 1024
LANES = 128


def _cparams(*sem):
    return pltpu.CompilerParams(dimension_semantics=sem, vmem_limit_bytes=V7X_VMEM_LIMIT_BYTES)


def _dot(a, b):
    return jnp.dot(a, b, preferred_element_type=F32)


def _dot_nt(a, b):
    return lax.dot_general(a, b, (((1,), (1,)), ((), ())), preferred_element_type=F32)


def _dot_tn(a, b):
    return lax.dot_general(a, b, (((0,), (0,)), ((), ())), preferred_element_type=F32)


def _sigmoid(x):
    return 1.0 / (1.0 + jnp.exp(-x))


def _layer_norm(x, g, b):
    mu = jnp.mean(x, axis=-1, keepdims=True)
    xc = x - mu
    var = jnp.mean(xc * xc, axis=-1, keepdims=True)
    return xc * lax.rsqrt(var + LN_EPS) * g + b


def _head_norm(x, g):
    mu = jnp.mean(x, axis=-1, keepdims=True)
    xc = x - mu
    var = jnp.mean(xc * xc, axis=-1, keepdims=True)
    return xc * lax.rsqrt(var + LN_EPS) * g


def _pick(n, *cands):
    for c in cands:
        if n % c == 0:
            return c
    return n


def _ln_cast_kernel(x_ref, g_ref, b_ref, o_ref):
    o_ref[...] = _layer_norm(x_ref[...], g_ref[...], b_ref[...]).astype(BF16)


def _ln_cast(x, g, b):
    n, d = x.shape
    tm = _pick(n, 256, 128)
    return pl.pallas_call(
        _ln_cast_kernel,
        out_shape=jax.ShapeDtypeStruct((n, d), BF16),
        grid=(n // tm,),
        in_specs=[pl.BlockSpec((tm, d), lambda i: (i, 0)),
                  pl.BlockSpec((1, d), lambda i: (0, 0)),
                  pl.BlockSpec((1, d), lambda i: (0, 0))],
        out_specs=pl.BlockSpec((tm, d), lambda i: (i, 0)),
        compiler_params=_cparams("parallel"),
        name="ln_cast",
    )(x, g.reshape(1, d), b.reshape(1, d))


def _in_proj_kernel(x_ref, w_ref, o_ref, wb_ref):
    @pl.when(pl.program_id(1) == 0)
    def _():
        wb_ref[...] = w_ref[...].astype(BF16)

    o_ref[...] = _dot(x_ref[...], wb_ref[...])


def _in_proj(xn, w_in, n_main):
    n, d = xn.shape
    tm = _pick(n, 1024, 512, 256, 128)
    tn = 512
    return pl.pallas_call(
        _in_proj_kernel,
        out_shape=jax.ShapeDtypeStruct((n, n_main), F32),
        grid=(n_main // tn, n // tm),
        in_specs=[pl.BlockSpec((tm, d), lambda j, i: (i, 0)),
                  pl.BlockSpec((d, tn), lambda j, i: (0, j))],
        out_specs=pl.BlockSpec((tm, tn), lambda j, i: (i, j)),
        scratch_shapes=[pltpu.VMEM((d, tn), BF16)],
        compiler_params=_cparams("parallel", "arbitrary"),
        name="in_proj",
    )(xn, w_in)


def _gate_rows_kernel(w_ref, x_ref, o_ref):
    o_ref[...] = _dot_nt(w_ref[...], x_ref[...])


def _gate_rows(xn, wg_t):
    n, d = xn.shape
    g = wg_t.shape[0]
    tb = _pick(n, 1024, 512, 256, 128)
    return pl.pallas_call(
        _gate_rows_kernel,
        out_shape=jax.ShapeDtypeStruct((g, n), F32),
        grid=(n // tb,),
        in_specs=[pl.BlockSpec((g, d), lambda i: (0, 0)),
                  pl.BlockSpec((tb, d), lambda i: (i, 0))],
        out_specs=pl.BlockSpec((g, tb), lambda i: (0, i)),
        compiler_params=_cparams("parallel"),
        name="gate_rows",
    )(wg_t, xn)


def _rope(x, cos, sin):
    half = x.shape[-1] // 2
    x1, x2 = x[:, :half], x[:, half:]
    return jnp.concatenate([x1 * cos - x2 * sin, x1 * sin + x2 * cos], axis=-1)


def _log_sigmoid(x):
    return jnp.minimum(x, 0.0) - jnp.log1p(jnp.exp(-jnp.abs(x)))


def _row_to_col(row, eye):
    return jnp.sum(jnp.where(eye, row, 0.0), axis=1, keepdims=True)


def _ret_prompt_kernel(lg_ref, q_ref, k_ref, v_ref, g_ref, cos_ref, sin_ref, gn_ref, o_ref, s_ref, *, chunk):
    h = pl.program_id(1)
    L = chunk
    t, dk = q_ref.shape
    dv = v_ref.shape[1]
    lg = lg_ref[h]
    ii = lax.broadcasted_iota(I32, (L, L), 0)
    jj = lax.broadcasted_iota(I32, (L, L), 1)
    causal = ii >= jj
    diff = jnp.where(causal, (ii - jj).astype(F32), 0.0)
    decay_in = jnp.where(causal, jnp.exp(lg * diff), 0.0)
    idx = lax.broadcasted_iota(I32, (L, 1), 0).astype(F32)
    decay_q = jnp.exp(lg * (idx + 1.0))
    decay_k = jnp.exp(lg * (float(L) - 1.0 - idx))
    decay_c = jnp.exp(lg * jnp.full((1, 1), float(L), F32))
    gn = gn_ref[pl.ds(h, 1), :]
    scale = dk ** -0.5
    s_ref[...] = jnp.zeros_like(s_ref)

    def body(c, carry):
        r0 = pl.multiple_of(c * L, L)
        rows = pl.ds(r0, L)
        cos, sin = cos_ref[rows, :], sin_ref[rows, :]
        rq = _rope(q_ref[rows, :], cos, sin)
        rk = _rope(k_ref[rows, :], cos, sin) * scale
        vb = v_ref[rows, :].astype(BF16)
        rqb = rq.astype(BF16)
        s = s_ref[...]
        sc = _dot_nt(rqb, rk.astype(BF16)) * decay_in
        o = _dot(sc.astype(BF16), vb) + _dot(rqb, s.astype(BF16)) * decay_q
        s_ref[...] = s * decay_c + _dot_tn((rk * decay_k).astype(BF16), vb)
        g = g_ref[rows, :]
        o_ref[rows, :] = (_head_norm(o, gn) * (g * _sigmoid(g))).astype(o_ref.dtype)
        return carry

    lax.fori_loop(0, t // L, body, 0)


def _ret_prompt(z, log_g, cos, sin, g_norm, bsz, t, heads, dk, col0):
    hb = col0 // dk
    zspec = lambda off: pl.BlockSpec((t, dk), lambda b, h, *_: (b, hb + off * heads + h))
    grid_spec = pltpu.PrefetchScalarGridSpec(
        num_scalar_prefetch=1,
        grid=(bsz, heads),
        in_specs=[zspec(0), zspec(1), zspec(2), zspec(3),
                  pl.BlockSpec((t, dk // 2), lambda b, h, *_: (0, 0)),
                  pl.BlockSpec((t, dk // 2), lambda b, h, *_: (0, 0)),
                  pl.BlockSpec((heads, dk), lambda b, h, *_: (0, 0))],
        out_specs=[pl.BlockSpec((t, dk), lambda b, h, *_: (b, h)),
                   pl.BlockSpec((None, None, dk, dk), lambda b, h, *_: (b, h, 0, 0))],
    )
    return pl.pallas_call(
        functools.partial(_ret_prompt_kernel, chunk=CHUNK),
        out_shape=[jax.ShapeDtypeStruct((bsz * t, heads * dk), BF16),
                   jax.ShapeDtypeStruct((bsz, heads, dk, dk), F32)],
        grid_spec=grid_spec,
        compiler_params=_cparams("parallel", "parallel"),
        name="ret_prompt",
    )(log_g, z, z, z, z, cos, sin, g_norm)


def _mlstm_prompt_kernel(bg_ref, xq_ref, xk_ref, v_ref, og_ref, ig_ref, fg_ref, cwq_ref, cwk_ref, cbq_ref, cbk_ref,
                         gn_ref, o_ref, c_ref, n_ref, m_ref, bt_s, ig_s, *, chunk, heads):
    h = pl.program_id(1)
    L = chunk
    t, dk = xq_ref.shape
    nc = t // L
    scale = dk ** -0.5

    ig_s[...] = ig_ref[...] + bg_ref[h]
    lf = _log_sigmoid(fg_ref[...] + bg_ref[heads + h])
    lane = lax.broadcasted_iota(I32, (nc, L), 1)
    bt = lf
    s = 1
    while s < L:
        bt = bt + jnp.where(lane >= s, pltpu.roll(bt, s, axis=1), 0.0)
        s *= 2
    bt_s[...] = bt

    ii = lax.broadcasted_iota(I32, (L, L), 0)
    jj = lax.broadcasted_iota(I32, (L, L), 1)
    causal = ii >= jj
    eye = ii == jj
    row = lax.broadcasted_iota(I32, (L, 1), 0)
    gn = gn_ref[pl.ds(h, 1), :]
    c_ref[...] = jnp.zeros_like(c_ref)
    n_ref[...] = jnp.zeros_like(n_ref)

    def conv_silu(x_ref, w_ref, b_ref, c, rows):
        x = x_ref[rows, :]
        prev_rows = pl.ds(pl.multiple_of(jnp.maximum(c - 1, 0) * L, L), L)
        xp = jnp.where(c > 0, x_ref[prev_rows, :], 0.0)
        y = x * w_ref[CONV_W - 1:CONV_W, :] + b_ref[...]
        for j in range(1, CONV_W):
            xs = jnp.where(row < j, pltpu.roll(xp, j, axis=0), pltpu.roll(x, j, axis=0))
            y = y + xs * w_ref[CONV_W - 1 - j:CONV_W - j, :]
        return y * _sigmoid(y)

    def body(c, m):
        rows = pl.ds(pl.multiple_of(c * L, L), L)
        q = conv_silu(xq_ref, cwq_ref, cbq_ref, c, rows)
        k = conv_silu(xk_ref, cwk_ref, cbk_ref, c, rows) * scale
        vb = v_ref[rows, :].astype(BF16)
        qb = q.astype(BF16)
        bt_row = bt_s[pl.ds(c, 1), :]
        ig_row = ig_s[pl.ds(c, 1), :]
        bt_col = _row_to_col(bt_row, eye)
        ig_col = _row_to_col(ig_row, eye)
        dmat = jnp.where(causal, bt_col - bt_row + ig_row, -jnp.inf)
        prior = bt_col + m
        mt = jnp.maximum(prior, jnp.max(dmat, axis=1, keepdims=True))
        w = jnp.exp(dmat - mt)
        wp = jnp.exp(prior - mt)
        qk = _dot_nt(qb, k.astype(BF16)) * w
        cst = c_ref[...]
        nst = n_ref[...]
        num = _dot(qk.astype(BF16), vb) + _dot(qb, cst.astype(BF16)) * wp
        den = jnp.sum(qk, axis=1, keepdims=True) + jnp.sum(q * nst, axis=1, keepdims=True) * wp
        hh = num / jnp.maximum(jnp.abs(den), jnp.exp(-mt))
        bl = bt_row[:, L - 1:L]
        m_new = mt[L - 1:L, :]
        wk = jnp.exp(bl - bt_col + ig_col - m_new)
        wc = jnp.exp(bl + m - m_new)
        kw = k * wk
        c_ref[...] = cst * wc + _dot_tn(kw.astype(BF16), vb)
        n_ref[...] = nst * wc + jnp.sum(kw, axis=0, keepdims=True)
        og = og_ref[rows, :]
        o_ref[rows, :] = (_head_norm(hh, gn) * _sigmoid(og)).astype(o_ref.dtype)
        return m_new

    m_fin = lax.fori_loop(0, nc, body, jnp.zeros((1, 1), F32))
    m_ref[...] = jnp.broadcast_to(m_fin, m_ref.shape)


def _mlstm_prompt(z, gates_t, b_gate, conv_w, conv_b, g_norm, bsz, t, heads, dk, col0):
    hb = col0 // dk
    nc = t // CHUNK
    zspec = lambda off: pl.BlockSpec((t, dk), lambda b, h, *_: (b, hb + off * heads + h))
    gates4 = gates_t.reshape(2 * heads, bsz, nc, CHUNK)
    gspec = lambda off: pl.BlockSpec((None, None, nc, CHUNK), lambda b, h, *_: (off * heads + h, b, 0, 0))
    wspec = lambda rows, off: pl.BlockSpec((rows, dk), lambda b, h, *_: (0, off * heads + h))
    grid_spec = pltpu.PrefetchScalarGridSpec(
        num_scalar_prefetch=1,
        grid=(bsz, heads),
        in_specs=[zspec(0), zspec(1), zspec(2), zspec(3), gspec(0), gspec(1),
                  wspec(CONV_W, 0), wspec(CONV_W, 1), wspec(1, 0), wspec(1, 1),
                  pl.BlockSpec((heads, dk), lambda b, h, *_: (0, 0))],
        out_specs=[pl.BlockSpec((t, dk), lambda b, h, *_: (b, h)),
                   pl.BlockSpec((None, None, dk, dk), lambda b, h, *_: (b, h, 0, 0)),
                   pl.BlockSpec((None, None, 1, dk), lambda b, h, *_: (b, h, 0, 0)),
                   pl.BlockSpec((None, None, 1, LANES), lambda b, h, *_: (b, h, 0, 0))],
        scratch_shapes=[pltpu.VMEM((nc, CHUNK), F32), pltpu.VMEM((nc, CHUNK), F32)],
    )
    o, c, n, m = pl.pallas_call(
        functools.partial(_mlstm_prompt_kernel, chunk=CHUNK, heads=heads),
        out_shape=[jax.ShapeDtypeStruct((bsz * t, heads * dk), BF16),
                   jax.ShapeDtypeStruct((bsz, heads, dk, dk), F32),
                   jax.ShapeDtypeStruct((bsz, heads, 1, dk), F32),
                   jax.ShapeDtypeStruct((bsz, heads, 1, LANES), F32)],
        grid_spec=grid_spec,
        compiler_params=_cparams("parallel", "parallel"),
        name="mlstm_prompt",
    )(b_gate, z, z, z, z, gates4, gates4, conv_w, conv_w, conv_b, conv_b, g_norm)
    return o, c, n[:, :, 0, :], m[:, :, 0, 0]


def _pad_rows(row, rows=8):
    r = lax.broadcasted_iota(I32, (rows, row.shape[1]), 0)
    return jnp.where(r == 0, row, 0.0)


def _mix_sample_kernel(lg_ref, bg_ref, z_ref, gt_ref, cs_ref, sr_ref, cb_ref, cc_ref, cn_ref, cm_ref, cw_ref, cbias_ref,
                       gr_ref, gm_ref, or_ref, om_ref, sro_ref, cbo_ref, cco_ref, cno_ref, cmo_ref, *, heads, dk):
    rw = heads * dk
    cos, sin = cs_ref[0:1, :], cs_ref[1:2, :]
    scale = dk ** -0.5
    xqk = z_ref[:, 4 * rw:6 * rw]
    buf = cb_ref[...]
    y = xqk * cw_ref[CONV_W - 1:CONV_W, :] + cbias_ref[...]
    for j in range(CONV_W - 1):
        y = y + buf[j:j + 1, :] * cw_ref[j:j + 1, :]
    qk_act = y * _sigmoid(y)
    cbo_ref[0:CONV_W - 2, :] = buf[1:CONV_W - 1, :]
    cbo_ref[CONV_W - 2:CONV_W - 1, :] = xqk

    for h in range(heads):
        sl = lambda g: slice(g * rw + h * dk, g * rw + (h + 1) * dk)
        gam = jnp.exp(jnp.full((1, 1), lg_ref[h], F32))
        rq = _rope(z_ref[:, sl(0)], cos, sin)
        rk = _rope(z_ref[:, sl(1)], cos, sin) * scale
        v = z_ref[:, sl(2)]
        rg = z_ref[:, sl(3)]
        s = sr_ref[h]
        sc = jnp.sum(rq * rk, axis=1, keepdims=True)
        qs = _dot(jnp.broadcast_to(rq, (8, dk)).astype(BF16), s.astype(BF16))[0:1, :]
        o = sc * v + qs * gam
        sro_ref[h] = s * gam + _dot_tn(_pad_rows(rk).astype(BF16), jnp.broadcast_to(v, (8, dk)).astype(BF16))
        or_ref[:, h * dk:(h + 1) * dk] = (_head_norm(o, gr_ref[h:h + 1, :]) * (rg * _sigmoid(rg))).astype(or_ref.dtype)
        q = qk_act[:, h * dk:(h + 1) * dk]
        k = qk_act[:, rw + h * dk:rw + (h + 1) * dk] * scale
        v = z_ref[:, sl(6)]
        og = z_ref[:, sl(7)]
        it = gt_ref[:, h:h + 1] + bg_ref[h]
        lf = _log_sigmoid(gt_ref[:, heads + h:heads + h + 1] + bg_ref[heads + h])
        m = cm_ref[:, h:h + 1]
        cst = cc_ref[h]
        nst = cn_ref[h:h + 1, :]
        prior = lf + m
        mt = jnp.maximum(prior, it)
        w = jnp.exp(it - mt)
        wp = jnp.exp(prior - mt)
        qk = jnp.sum(q * k, axis=1, keepdims=True) * w
        qc = _dot(jnp.broadcast_to(q, (8, dk)).astype(BF16), cst.astype(BF16))[0:1, :]
        num = qk * v + qc * wp
        den = qk + jnp.sum(q * nst, axis=1, keepdims=True) * wp
        hh = num / jnp.maximum(jnp.abs(den), jnp.exp(-mt))
        wk = jnp.exp(it - mt)
        wc = jnp.exp(lf + m - mt)
        kw = k * wk
        cco_ref[h] = cst * wc + _dot_tn(_pad_rows(kw).astype(BF16), jnp.broadcast_to(v, (8, dk)).astype(BF16))
        cno_ref[h:h + 1, :] = nst * wc + kw
        cmo_ref[:, h:h + 1] = mt
        om_ref[:, h * dk:(h + 1) * dk] = (_head_norm(hh, gm_ref[h:h + 1, :]) * _sigmoid(og)).astype(om_ref.dtype)


def _mix_sample(z, gates, log_g, b_gate, cos_sin, s_ret, s_conv, s_c, s_n, s_m, conv_w, conv_b, g_ret, g_ml, heads, dk):
    nb = z.shape[0]
    rw = heads * dk
    per_b3 = lambda *tail: pl.BlockSpec((None,) + tail, lambda b, *_: (b,) + (0,) * len(tail))
    whole = lambda a: pl.BlockSpec(a.shape, lambda b, *_: (0,) * a.ndim)
    z3 = z.reshape(nb, 1, 8 * rw)
    g3 = gates.reshape(nb, 1, 2 * heads)
    m3 = s_m.reshape(nb, 1, heads)
    cb2 = conv_b.reshape(1, 2 * rw)
    grid_spec = pltpu.PrefetchScalarGridSpec(
        num_scalar_prefetch=2,
        grid=(nb,),
        in_specs=[per_b3(1, 8 * rw), per_b3(1, 2 * heads), whole(cos_sin),
                  per_b3(heads, dk, dk), per_b3(CONV_W - 1, 2 * rw), per_b3(heads, dk, dk), per_b3(heads, dk),
                  per_b3(1, heads), whole(conv_w), whole(cb2), whole(g_ret), whole(g_ml)],
        out_specs=[per_b3(1, rw), per_b3(1, rw), per_b3(heads, dk, dk), per_b3(CONV_W - 1, 2 * rw),
                   per_b3(heads, dk, dk), per_b3(heads, dk), per_b3(1, heads)],
    )
    o_r, o_m, sr, cb, cc, cn, cm = pl.pallas_call(
        functools.partial(_mix_sample_kernel, heads=heads, dk=dk),
        out_shape=[jax.ShapeDtypeStruct((nb, 1, rw), BF16), jax.ShapeDtypeStruct((nb, 1, rw), BF16),
                   jax.ShapeDtypeStruct(s_ret.shape, F32), jax.ShapeDtypeStruct(s_conv.shape, F32),
                   jax.ShapeDtypeStruct(s_c.shape, F32), jax.ShapeDtypeStruct(s_n.shape, F32),
                   jax.ShapeDtypeStruct((nb, 1, heads), F32)],
        grid_spec=grid_spec,
        compiler_params=_cparams("parallel"),
        name="mix_sample",
    )(log_g, b_gate, z3, g3, cos_sin, s_ret, s_conv, s_c, s_n, m3, conv_w, cb2, g_ret, g_ml)
    return o_r.reshape(nb, rw), o_m.reshape(nb, rw), sr, cb, cc, cn, cm.reshape(nb, heads)


def _out_proj_kernel(a_ref, b_ref, wa_ref, wb_ref, x_ref, eg_ref, eb_ref, g_ref, bb_ref, of_ref, ob_ref, *, tn):
    j = pl.program_id(1)
    cols = pl.ds(pl.multiple_of(j * tn, tn), tn)
    of_ref[:, cols] = _dot(a_ref[...], wa_ref[...]) + _dot(b_ref[...], wb_ref[...])

    @pl.when(j == pl.num_programs(1) - 1)
    def _():
        xe = _layer_norm(x_ref[...], eg_ref[...], eb_ref[...])
        x1 = _layer_norm(DEEPNORM_ALPHA * xe + of_ref[...], g_ref[...], bb_ref[...])
        of_ref[...] = x1
        ob_ref[...] = x1.astype(BF16)


def _out_proj_ln1(o_r, o_m, w_out_b, x_raw, eg, eb, g, b):
    n, d = x_raw.shape
    ka = o_r.shape[1]
    tm = _pick(n, 256, 128)
    tn = 512
    row = lambda a: a.reshape(1, d)
    vec = pl.BlockSpec((1, d), lambda i, j: (0, 0))
    return pl.pallas_call(
        functools.partial(_out_proj_kernel, tn=tn),
        out_shape=[jax.ShapeDtypeStruct((n, d), F32), jax.ShapeDtypeStruct((n, d), BF16)],
        grid=(n // tm, d // tn),
        in_specs=[pl.BlockSpec((tm, ka), lambda i, j: (i, 0)),
                  pl.BlockSpec((tm, ka), lambda i, j: (i, 0)),
                  pl.BlockSpec((ka, tn), lambda i, j: (0, j)),
                  pl.BlockSpec((ka, tn), lambda i, j: (1, j)),
                  pl.BlockSpec((tm, d), lambda i, j: (i, 0)),
                  vec, vec, vec, vec],
        out_specs=[pl.BlockSpec((tm, d), lambda i, j: (i, 0)),
                   pl.BlockSpec((tm, d), lambda i, j: (i, 0))],
        compiler_params=_cparams("parallel", "arbitrary"),
        name="out_proj_ln1",
    )(o_r, o_m, w_out_b, w_out_b, x_raw, row(eg), row(eb), row(g), row(b))


def _peer_scores_kernel(x_ref, wq_ref, keys_ref, o_ref):
    q = _dot(x_ref[...], wq_ref[...])
    nk, kd = keys_ref.shape[1], keys_ref.shape[2]
    for hp in range(keys_ref.shape[0]):
        qh = q[:, hp * kd:(hp + 1) * kd].astype(BF16)
        o_ref[hp] = _dot_nt(keys_ref[hp], qh)


def _peer_scores(x1b, wq_b, keys_b):
    n, d = x1b.shape
    hp, nk, kd = keys_b.shape
    tb = _pick(n, 512, 256, 128)
    return pl.pallas_call(
        _peer_scores_kernel,
        out_shape=jax.ShapeDtypeStruct((hp, nk, n), F32),
        grid=(n // tb,),
        in_specs=[pl.BlockSpec((tb, d), lambda i: (i, 0)),
                  pl.BlockSpec(wq_b.shape, lambda i: (0, 0), pipeline_mode=pl.Buffered(1)),
                  pl.BlockSpec(keys_b.shape, lambda i: (0, 0, 0))],
        out_specs=pl.BlockSpec((hp, nk, tb), lambda i: (0, 0, i)),
        compiler_params=_cparams("parallel"),
        name="peer_scores",
    )(x1b, wq_b, keys_b)


def _top_ranks(s, k):
    n, tb = s.shape
    idx = lax.broadcasted_iota(I32, (n, tb), 0)
    rank = jnp.full((n, tb), k, I32)
    vals = []
    for r in range(k):
        m = jnp.max(s, axis=0, keepdims=True)
        first = jnp.min(jnp.where(s == m, idx, n), axis=0, keepdims=True)
        hit = idx == first
        rank = jnp.where(hit, r, rank)
        s = jnp.where(hit, -jnp.inf, s)
        vals.append(m)
    return jnp.concatenate(vals, axis=0), rank


def _peer_route_kernel(s_ref, e1_ref, rb_ref, e2_ref, b2_ref, *, topk):
    heads = e1_ref.shape[0]
    tb = s_ref.shape[2]
    bit = jnp.left_shift(1, lax.broadcasted_iota(I32, (topk, tb), 0))
    for h in range(heads):
        s1 = s_ref[2 * h]
        s2 = s_ref[2 * h + 1]
        a, r1 = _top_ranks(s1, topk)
        b, r2 = _top_ranks(s2, topk)
        cand = jnp.concatenate([a[r:r + 1, :] + b for r in range(topk)], axis=0)
        tv, rc = _top_ranks(cand, topk)
        z = jnp.sum(jnp.exp(tv - tv[0:1, :]), axis=0, keepdims=True)
        rb = jnp.zeros(s1.shape, I32)
        for r in range(topk):
            sel = rc[r * topk:(r + 1) * topk, :] < topk
            rowbits = jnp.sum(jnp.where(sel, bit, 0), axis=0, keepdims=True)
            rb = jnp.where(r1 == r, rowbits, rb)
        e1_ref[h] = jnp.where(r1 < topk, jnp.exp(s1 - a[0:1, :]) / z, 0.0)
        rb_ref[h] = rb
        e2_ref[h] = jnp.where(r2 < topk, jnp.exp(s2 - b[0:1, :]), 0.0)
        b2_ref[h] = jnp.where(r2 < topk, jnp.left_shift(1, jnp.minimum(r2, topk - 1)), 0)


def _peer_route(scores, heads):
    hp, nk, n = scores.shape
    tb = LANES
    spec = pl.BlockSpec((heads, nk, tb), lambda i: (0, 0, i))
    return pl.pallas_call(
        functools.partial(_peer_route_kernel, topk=PEER_TOPK),
        out_shape=[jax.ShapeDtypeStruct((heads, nk, n), F32), jax.ShapeDtypeStruct((heads, nk, n), I32),
                   jax.ShapeDtypeStruct((heads, nk, n), F32), jax.ShapeDtypeStruct((heads, nk, n), I32)],
        grid=(n // tb,),
        in_specs=[pl.BlockSpec((hp, nk, tb), lambda i: (0, 0, i))],
        out_specs=[spec, spec, spec, spec],
        compiler_params=_cparams("parallel"),
        name="peer_route",
    )(scores)


def _gelu_tanh(x):
    c = math.sqrt(2.0 / math.pi)
    return x * (0.5 * (1.0 + jnp.tanh(c * (x + 0.044715 * (x * x * x)))))


def _peer_dense_kernel(x_ref, u_ref, v_ref, e1_ref, rb_ref, e2_ref, b2_ref, y_ref, *, ti):
    j = pl.program_id(1)
    heads, nk, tb = e2_ref.shape

    @pl.when(j == 0)
    def _():
        y_ref[...] = jnp.zeros_like(y_ref)

    act = _gelu_tanh(_dot_nt(u_ref[...], x_ref[...]))
    parts = []
    for t in range(ti):
        g = jnp.zeros((nk, tb), F32)
        for h in range(heads):
            i1 = pl.ds(j * ti + t, 1)
            hit = (rb_ref[h, i1, :] & b2_ref[h]) != 0
            g = g + jnp.where(hit, e1_ref[h, i1, :] * e2_ref[h], 0.0)
        parts.append((act[t * nk:(t + 1) * nk, :] * g).astype(BF16))
    y_ref[...] += _dot_tn(jnp.concatenate(parts, axis=0), v_ref[...])


def _peer_dense(x1b, u_b, v_b, e1, rb, e2, b2):
    n, d = x1b.shape
    ne = u_b.shape[0]
    heads, nk, _ = e1.shape
    tb = _pick(n, 512, 256, 128)
    te = 512
    ti = te // nk
    once = pl.Buffered(1)
    rspec = pl.BlockSpec((heads, nk, tb), lambda i, j: (0, 0, i), pipeline_mode=once)
    return pl.pallas_call(
        functools.partial(_peer_dense_kernel, ti=ti),
        out_shape=jax.ShapeDtypeStruct((n, d), F32),
        grid=(n // tb, ne // te),
        in_specs=[pl.BlockSpec((tb, d), lambda i, j: (i, 0), pipeline_mode=once),
                  pl.BlockSpec((te, d), lambda i, j: (j, 0)),
                  pl.BlockSpec((te, d), lambda i, j: (j, 0)),
                  rspec, rspec, rspec, rspec],
        out_specs=pl.BlockSpec((tb, d), lambda i, j: (i, 0)),
        compiler_params=_cparams("parallel", "arbitrary"),
        name="peer_dense",
    )(x1b, u_b, v_b, e1, rb, e2, b2)


def _final_kernel(xf_ref, xb_ref, ch_ref, p_ref, wg_ref, wp_ref, g_ref, b_ref, o_ref, *, tn):
    j = pl.program_id(1)
    cols = pl.ds(pl.multiple_of(j * tn, tn), tn)
    gate = _sigmoid(_dot(xb_ref[...], wg_ref[...]))
    proj = _dot(p_ref[...].astype(BF16), wp_ref[...])
    o_ref[:, cols] = DEEPNORM_ALPHA * xf_ref[...] + ch_ref[...] + gate * proj

    @pl.when(j == pl.num_programs(1) - 1)
    def _():
        o_ref[...] = _layer_norm(o_ref[...], g_ref[...], b_ref[...])


def _final(x1f, x1b, ch, p, wg_b, wp_b, g, b):
    n, d = x1f.shape
    pd = p.shape[1]
    tm = _pick(n, 256, 128)
    tn = 512
    vec = pl.BlockSpec((1, d), lambda i, j: (0, 0))
    rows = lambda w: pl.BlockSpec((tm, w), lambda i, j: (i, 0))
    tile = pl.BlockSpec((tm, tn), lambda i, j: (i, j))
    return pl.pallas_call(
        functools.partial(_final_kernel, tn=tn),
        out_shape=jax.ShapeDtypeStruct((n, d), F32),
        grid=(n // tm, d // tn),
        in_specs=[tile, rows(d), tile, rows(pd),
                  pl.BlockSpec((d, tn), lambda i, j: (0, j)),
                  pl.BlockSpec((pd, tn), lambda i, j: (0, j)),
                  vec, vec],
        out_specs=rows(d),
        compiler_params=_cparams("parallel", "arbitrary"),
        name="ple_ln2",
    )(x1f, x1b, ch, p, wg_b, wp_b, g.reshape(1, d), b.reshape(1, d))


def _rope_tables(pos, half):
    inv = ROPE_BASE ** (-jnp.arange(half, dtype=F32) / half)
    ang = pos.astype(F32)[:, None] * inv[None]
    return jnp.cos(ang), jnp.sin(ang)


def _post_mixer(o_r, o_m, x_raw, p, wts):
    x1f, x1b = _out_proj_ln1(o_r, o_m, wts["w_out"], x_raw, wts["ln_emb_g"], wts["ln_emb_b"], wts["ln1_g"], wts["ln1_b"])
    heads = wts["keys"].shape[0] // 2
    scores = _peer_scores(x1b, wts["w_q"], wts["keys"])
    e1, rb, e2, b2 = _peer_route(scores, heads)
    ch = _peer_dense(x1b, wts["u"], wts["v"], e1, rb, e2, b2)
    return _final(x1f, x1b, ch, p, wts["w_gate"], wts["w_proj"], wts["ln2_g"], wts["ln2_b"])


def kernel(x_prompt, x_sample, state_ret, state_conv, state_mlstm_c, state_mlstm_n, state_mlstm_m, p_prompt, p_sample,
           ln_emb_g, ln_emb_b, w_in, b_gate, conv_w, conv_b, g_ret_norm, g_ml_norm, w_out, ln1_g, ln1_b,
           w_peer_q, peer_sub_keys, peer_u, peer_v, w_ple_gate, w_ple_proj, ln2_g, ln2_b):
    bsz, t, d = x_prompt.shape
    nb = x_sample.shape[0]
    _, _, heads, dk, _ = state_ret.shape
    rw = heads * dk
    n_main = 8 * rw
    assert x_sample.shape[1] == 1 and w_in.shape[0] == DEPTH and t % CHUNK == 0

    w_in0 = w_in[0]
    wg_t = w_in0[:, n_main:].T.astype(BF16)
    log_g = jnp.log1p(-(2.0 ** (-5.0 - jnp.arange(heads, dtype=F32))))
    keys = peer_sub_keys[0]
    wts = dict(
        ln_emb_g=ln_emb_g, ln_emb_b=ln_emb_b, ln1_g=ln1_g[0], ln1_b=ln1_b[0], ln2_g=ln2_g[0], ln2_b=ln2_b[0],
        w_out=w_out[0].astype(BF16), w_q=w_peer_q[0].astype(BF16),
        keys=keys.reshape(keys.shape[0] * 2, keys.shape[2], keys.shape[3]).astype(BF16),
        u=peer_u[0].astype(BF16), v=peer_v[0].astype(BF16),
        w_gate=w_ple_gate[0].astype(BF16), w_proj=w_ple_proj[0].astype(BF16),
    )

    xp = x_prompt.reshape(bsz * t, d)
    xn = _ln_cast(xp, ln_emb_g, ln_emb_b)
    z = _in_proj(xn, w_in0, n_main)
    gates_t = _gate_rows(xn, wg_t)
    cos, sin = _rope_tables(jnp.arange(t), dk // 2)
    o_r, ret_p = _ret_prompt(z, log_g, cos, sin, g_ret_norm[0], bsz, t, heads, dk, 0)
    o_m, c_p, n_p, m_p = _mlstm_prompt(z, gates_t, b_gate[0], conv_w[0], conv_b[0].reshape(1, 2 * rw), g_ml_norm[0],
                                       bsz, t, heads, dk, 4 * rw)
    conv_p = z.reshape(bsz, t, n_main)[:, t - (CONV_W - 1):, 4 * rw:6 * rw]
    y_prompt = _post_mixer(o_r, o_m, xp, p_prompt[0].reshape(bsz * t, -1), wts).reshape(bsz, t, d)

    xs = x_sample.reshape(nb, d)
    xns = _ln_cast(xs, ln_emb_g, ln_emb_b)
    zs = _in_proj(xns, w_in0, n_main)
    gates_s = _gate_rows(xns, wg_t).T
    cs, sn = _rope_tables(jnp.full((1,), PAST_LEN), dk // 2)
    o_rs, o_ms, ret_s, conv_s, c_s, n_s, m_s = _mix_sample(
        zs, gates_s, log_g, b_gate[0], jnp.concatenate([cs, sn], axis=0), state_ret[0], state_conv[0],
        state_mlstm_c[0], state_mlstm_n[0], state_mlstm_m[0], conv_w[0], conv_b[0], g_ret_norm[0], g_ml_norm[0], heads, dk)
    y_sample = _post_mixer(o_rs, o_ms, xs, p_sample[0].reshape(nb, -1), wts).reshape(nb, 1, d)

    lead = lambda a: a[None]
    return (y_prompt, y_sample, lead(ret_p), lead(conv_p), lead(c_p), lead(n_p), lead(m_p),
            lead(ret_s), lead(conv_s), lead(c_s), lead(n_s), lead(m_s))
```

```python
import functools
import math

import jax
import jax.numpy as jnp
from jax import lax
from jax.experimental import pallas as pl
from jax.experimental.pallas import tpu as pltpu

F32 = jnp.float32
BF16 = jnp.bfloat16
I32 = jnp.int32

LN_EPS = 1e-5
DEPTH = 1
DEEPNORM_ALPHA = (2.0 * DEPTH) ** 0.25
CHUNK = 128
ROPE_BASE = 10000.0
PAST_LEN = 16384
PEER_TOPK = 16
CONV_W = 4

V7X_VMEM_LIMIT_BYTES = 56 * 1024 * 1024
LANES = 128


def _cparams(*sem):
    return pltpu.CompilerParams(dimension_semantics=sem, vmem_limit_bytes=V7X_VMEM_LIMIT_BYTES)


def _dot(a, b):
    return jnp.dot(a, b, preferred_element_type=F32)


def _dot_nt(a, b):
    return lax.dot_general(a, b, (((1,), (1,)), ((), ())), preferred_element_type=F32)


def _dot_tn(a, b):
    return lax.dot_general(a, b, (((0,), (0,)), ((), ())), preferred_element_type=F32)


def _sigmoid(x):
    return 1.0 / (1.0 + jnp.exp(-x))


def _layer_norm(x, g, b):
    mu = jnp.mean(x, axis=-1, keepdims=True)
    xc = x - mu
    var = jnp.mean(xc * xc, axis=-1, keepdims=True)
    return xc * lax.rsqrt(var + LN_EPS) * g + b


def _head_norm(x, g):
    mu = jnp.mean(x, axis=-1, keepdims=True)
    xc = x - mu
    var = jnp.mean(xc * xc, axis=-1, keepdims=True)
    return xc * lax.rsqrt(var + LN_EPS) * g


def _pick(n, *cands):
    for c in cands:
        if n % c == 0:
            return c
    return n


def _ln_cast_kernel(x_ref, g_ref, b_ref, o_ref):
    o_ref[...] = _layer_norm(x_ref[...], g_ref[...], b_ref[...]).astype(BF16)


def _ln_cast(x, g, b):
    n, d = x.shape
    tm = _pick(n, 256, 128)
    return pl.pallas_call(
        _ln_cast_kernel,
        out_shape=jax.ShapeDtypeStruct((n, d), BF16),
        grid=(n // tm,),
        in_specs=[pl.BlockSpec((tm, d), lambda i: (i, 0)),
                  pl.BlockSpec((1, d), lambda i: (0, 0)),
                  pl.BlockSpec((1, d), lambda i: (0, 0))],
        out_specs=pl.BlockSpec((tm, d), lambda i: (i, 0)),
        compiler_params=_cparams("parallel"),
        name="ln_cast",
    )(x, g.reshape(1, d), b.reshape(1, d))


def _in_proj_kernel(x_ref, w_ref, o_ref, wb_ref):
    @pl.when(pl.program_id(1) == 0)
    def _():
        wb_ref[...] = w_ref[...].astype(BF16)

    o_ref[...] = _dot(x_ref[...], wb_ref[...])


def _in_proj(xn, w_in, n_main):
    n, d = xn.shape
    tm = _pick(n, 1024, 512, 256, 128)
    tn = 512
    return pl.pallas_call(
        _in_proj_kernel,
        out_shape=jax.ShapeDtypeStruct((n, n_main), F32),
        grid=(n_main // tn, n // tm),
        in_specs=[pl.BlockSpec((tm, d), lambda j, i: (i, 0)),
                  pl.BlockSpec((d, tn), lambda j, i: (0, j))],
        out_specs=pl.BlockSpec((tm, tn), lambda j, i: (i, j)),
        scratch_shapes=[pltpu.VMEM((d, tn), BF16)],
        compiler_params=_cparams("parallel", "arbitrary"),
        name="in_proj",
    )(xn, w_in)


def _gate_rows_kernel(w_ref, x_ref, o_ref):
    o_ref[...] = _dot_nt(w_ref[...], x_ref[...])


def _gate_rows(xn, wg_t):
    n, d = xn.shape
    g = wg_t.shape[0]
    tb = _pick(n, 1024, 512, 256, 128)
    return pl.pallas_call(
        _gate_rows_kernel,
        out_shape=jax.ShapeDtypeStruct((g, n), F32),
        grid=(n // tb,),
        in_specs=[pl.BlockSpec((g, d), lambda i: (0, 0)),
                  pl.BlockSpec((tb, d), lambda i: (i, 0))],
        out_specs=pl.BlockSpec((g, tb), lambda i: (0, i)),
        compiler_params=_cparams("parallel"),
        name="gate_rows",
    )(wg_t, xn)


def _rope(x, cos, sin):
    half = x.shape[-1] // 2
    x1, x2 = x[:, :half], x[:, half:]
    return jnp.concatenate([x1 * cos - x2 * sin, x1 * sin + x2 * cos], axis=-1)


def _log_sigmoid(x):
    return jnp.minimum(x, 0.0) - jnp.log1p(jnp.exp(-jnp.abs(x)))


def _row_to_col(row, eye):
    return jnp.sum(jnp.where(eye, row, 0.0), axis=1, keepdims=True)


def _ret_prompt_kernel(lg_ref, q_ref, k_ref, v_ref, g_ref, cos_ref, sin_ref, gn_ref, o_ref, s_ref, *, chunk):
    h = pl.program_id(1)
    L = chunk
    t, dk = q_ref.shape
    dv = v_ref.shape[1]
    lg = lg_ref[h]
    ii = lax.broadcasted_iota(I32, (L, L), 0)
    jj = lax.broadcasted_iota(I32, (L, L), 1)
    causal = ii >= jj
    diff = jnp.where(causal, (ii - jj).astype(F32), 0.0)
    decay_in = jnp.where(causal, jnp.exp(lg * diff), 0.0)
    idx = lax.broadcasted_iota(I32, (L, 1), 0).astype(F32)
    decay_q = jnp.exp(lg * (idx + 1.0))
    decay_k = jnp.exp(lg * (float(L) - 1.0 - idx))
    decay_c = jnp.exp(lg * jnp.full((1, 1), float(L), F32))
    gn = gn_ref[pl.ds(h, 1), :]
    scale = dk ** -0.5
    s_ref[...] = jnp.zeros_like(s_ref)

    def body(c, carry):
        r0 = pl.multiple_of(c * L, L)
        rows = pl.ds(r0, L)
        cos, sin = cos_ref[rows, :], sin_ref[rows, :]
        rq = _rope(q_ref[rows, :], cos, sin)
        rk = _rope(k_ref[rows, :], cos, sin) * scale
        vb = v_ref[rows, :].astype(BF16)
        rqb = rq.astype(BF16)
        s = s_ref[...]
        sc = _dot_nt(rqb, rk.astype(BF16)) * decay_in
        o = _dot(sc.astype(BF16), vb) + _dot(rqb, s.astype(BF16)) * decay_q
        s_ref[...] = s * decay_c + _dot_tn((rk * decay_k).astype(BF16), vb)
        g = g_ref[rows, :]
        o_ref[rows, :] = (_head_norm(o, gn) * (g * _sigmoid(g))).astype(o_ref.dtype)
        return carry

    lax.fori_loop(0, t // L, body, 0)


def _ret_prompt(z, log_g, cos, sin, g_norm, bsz, t, heads, dk, col0):
    hb = col0 // dk
    zspec = lambda off: pl.BlockSpec((t, dk), lambda b, h, *_: (b, hb + off * heads + h))
    grid_spec = pltpu.PrefetchScalarGridSpec(
        num_scalar_prefetch=1,
        grid=(bsz, heads),
        in_specs=[zspec(0), zspec(1), zspec(2), zspec(3),
                  pl.BlockSpec((t, dk // 2), lambda b, h, *_: (0, 0)),
                  pl.BlockSpec((t, dk // 2), lambda b, h, *_: (0, 0)),
                  pl.BlockSpec((heads, dk), lambda b, h, *_: (0, 0))],
        out_specs=[pl.BlockSpec((t, dk), lambda b, h, *_: (b, h)),
                   pl.BlockSpec((None, None, dk, dk), lambda b, h, *_: (b, h, 0, 0))],
    )
    return pl.pallas_call(
        functools.partial(_ret_prompt_kernel, chunk=CHUNK),
        out_shape=[jax.ShapeDtypeStruct((bsz * t, heads * dk), BF16),
                   jax.ShapeDtypeStruct((bsz, heads, dk, dk), F32)],
        grid_spec=grid_spec,
        compiler_params=_cparams("parallel", "parallel"),
        name="ret_prompt",
    )(log_g, z, z, z, z, cos, sin, g_norm)


def _mlstm_prompt_kernel(bg_ref, xq_ref, xk_ref, v_ref, og_ref, ig_ref, fg_ref, cwq_ref, cwk_ref, cbq_ref, cbk_ref,
                         gn_ref, o_ref, c_ref, n_ref, m_ref, bt_s, ig_s, *, chunk, heads):
    h = pl.program_id(1)
    L = chunk
    t, dk = xq_ref.shape
    nc = t // L
    scale = dk ** -0.5

    ig_s[...] = ig_ref[...] + bg_ref[h]
    lf = _log_sigmoid(fg_ref[...] + bg_ref[heads + h])
    lane = lax.broadcasted_iota(I32, (nc, L), 1)
    bt = lf
    s = 1
    while s < L:
        bt = bt + jnp.where(lane >= s, pltpu.roll(bt, s, axis=1), 0.0)
        s *= 2
    bt_s[...] = bt

    ii = lax.broadcasted_iota(I32, (L, L), 0)
    jj = lax.broadcasted_iota(I32, (L, L), 1)
    causal = ii >= jj
    eye = ii == jj
    row = lax.broadcasted_iota(I32, (L, 1), 0)
    gn = gn_ref[pl.ds(h, 1), :]
    c_ref[...] = jnp.zeros_like(c_ref)
    n_ref[...] = jnp.zeros_like(n_ref)

    def conv_silu(x_ref, w_ref, b_ref, c, rows):
        x = x_ref[rows, :]
        prev_rows = pl.ds(pl.multiple_of(jnp.maximum(c - 1, 0) * L, L), L)
        xp = jnp.where(c > 0, x_ref[prev_rows, :], 0.0)
        y = x * w_ref[CONV_W - 1:CONV_W, :] + b_ref[...]
        for j in range(1, CONV_W):
            xs = jnp.where(row < j, pltpu.roll(xp, j, axis=0), pltpu.roll(x, j, axis=0))
            y = y + xs * w_ref[CONV_W - 1 - j:CONV_W - j, :]
        return y * _sigmoid(y)

    def body(c, m):
        rows = pl.ds(pl.multiple_of(c * L, L), L)
        q = conv_silu(xq_ref, cwq_ref, cbq_ref, c, rows)
        k = conv_silu(xk_ref, cwk_ref, cbk_ref, c, rows) * scale
        vb = v_ref[rows, :].astype(BF16)
        qb = q.astype(BF16)
        bt_row = bt_s[pl.ds(c, 1), :]
        ig_row = ig_s[pl.ds(c, 1), :]
        bt_col = _row_to_col(bt_row, eye)
        ig_col = _row_to_col(ig_row, eye)
        dmat = jnp.where(causal, bt_col - bt_row + ig_row, -jnp.inf)
        prior = bt_col + m
        mt = jnp.maximum(prior, jnp.max(dmat, axis=1, keepdims=True))
        w = jnp.exp(dmat - mt)
        wp = jnp.exp(prior - mt)
        qk = _dot_nt(qb, k.astype(BF16)) * w
        cst = c_ref[...]
        nst = n_ref[...]
        num = _dot(qk.astype(BF16), vb) + _dot(qb, cst.astype(BF16)) * wp
        den = jnp.sum(qk, axis=1, keepdims=True) + jnp.sum(q * nst, axis=1, keepdims=True) * wp
        hh = num / jnp.maximum(jnp.abs(den), jnp.exp(-mt))
        bl = bt_row[:, L - 1:L]
        m_new = mt[L - 1:L, :]
        wk = jnp.exp(bl - bt_col + ig_col - m_new)
        wc = jnp.exp(bl + m - m_new)
        kw = k * wk
        c_ref[...] = cst * wc + _dot_tn(kw.astype(BF16), vb)
        n_ref[...] = nst * wc + jnp.sum(kw, axis=0, keepdims=True)
        og = og_ref[rows, :]
        o_ref[rows, :] = (_head_norm(hh, gn) * _sigmoid(og)).astype(o_ref.dtype)
        return m_new

    m_fin = lax.fori_loop(0, nc, body, jnp.zeros((1, 1), F32))
    m_ref[...] = jnp.broadcast_to(m_fin, m_ref.shape)


def _mlstm_prompt(z, gates_t, b_gate, conv_w, conv_b, g_norm, bsz, t, heads, dk, col0):
    hb = col0 // dk
    nc = t // CHUNK
    zspec = lambda off: pl.BlockSpec((t, dk), lambda b, h, *_: (b, hb + off * heads + h))
    gates4 = gates_t.reshape(2 * heads, bsz, nc, CHUNK)
    gspec = lambda off: pl.BlockSpec((None, None, nc, CHUNK), lambda b, h, *_: (off * heads + h, b, 0, 0))
    wspec = lambda rows, off: pl.BlockSpec((rows, dk), lambda b, h, *_: (0, off * heads + h))
    grid_spec = pltpu.PrefetchScalarGridSpec(
        num_scalar_prefetch=1,
        grid=(bsz, heads),
        in_specs=[zspec(0), zspec(1), zspec(2), zspec(3), gspec(0), gspec(1),
                  wspec(CONV_W, 0), wspec(CONV_W, 1), wspec(1, 0), wspec(1, 1),
                  pl.BlockSpec((heads, dk), lambda b, h, *_: (0, 0))],
        out_specs=[pl.BlockSpec((t, dk), lambda b, h, *_: (b, h)),
                   pl.BlockSpec((None, None, dk, dk), lambda b, h, *_: (b, h, 0, 0)),
                   pl.BlockSpec((None, None, 1, dk), lambda b, h, *_: (b, h, 0, 0)),
                   pl.BlockSpec((None, None, 1, LANES), lambda b, h, *_: (b, h, 0, 0))],
        scratch_shapes=[pltpu.VMEM((nc, CHUNK), F32), pltpu.VMEM((nc, CHUNK), F32)],
    )
    o, c, n, m = pl.pallas_call(
        functools.partial(_mlstm_prompt_kernel, chunk=CHUNK, heads=heads),
        out_shape=[jax.ShapeDtypeStruct((bsz * t, heads * dk), BF16),
                   jax.ShapeDtypeStruct((bsz, heads, dk, dk), F32),
                   jax.ShapeDtypeStruct((bsz, heads, 1, dk), F32),
                   jax.ShapeDtypeStruct((bsz, heads, 1, LANES), F32)],
        grid_spec=grid_spec,
        compiler_params=_cparams("parallel", "parallel"),
        name="mlstm_prompt",
    )(b_gate, z, z, z, z, gates4, gates4, conv_w, conv_w, conv_b, conv_b, g_norm)
    return o, c, n[:, :, 0, :], m[:, :, 0, 0]


def _pad_rows(row, rows=8):
    r = lax.broadcasted_iota(I32, (rows, row.shape[1]), 0)
    return jnp.where(r == 0, row, 0.0)


def _mix_sample_kernel(lg_ref, bg_ref, z_ref, gt_ref, cs_ref, sr_ref, cb_ref, cc_ref, cn_ref, cm_ref, cw_ref, cbias_ref,
                       gr_ref, gm_ref, or_ref, om_ref, sro_ref, cbo_ref, cco_ref, cno_ref, cmo_ref, *, heads, dk):
    rw = heads * dk
    cos, sin = cs_ref[0:1, :], cs_ref[1:2, :]
    scale = dk ** -0.5
    xqk = z_ref[:, 4 * rw:6 * rw]
    buf = cb_ref[...]
    y = xqk * cw_ref[CONV_W - 1:CONV_W, :] + cbias_ref[...]
    for j in range(CONV_W - 1):
        y = y + buf[j:j + 1, :] * cw_ref[j:j + 1, :]
    qk_act = y * _sigmoid(y)
    cbo_ref[0:CONV_W - 2, :] = buf[1:CONV_W - 1, :]
    cbo_ref[CONV_W - 2:CONV_W - 1, :] = xqk

    for h in range(heads):
        sl = lambda g: slice(g * rw + h * dk, g * rw + (h + 1) * dk)
        gam = jnp.exp(jnp.full((1, 1), lg_ref[h], F32))
        rq = _rope(z_ref[:, sl(0)], cos, sin)
        rk = _rope(z_ref[:, sl(1)], cos, sin) * scale
        v = z_ref[:, sl(2)]
        rg = z_ref[:, sl(3)]
        s = sr_ref[h]
        sc = jnp.sum(rq * rk, axis=1, keepdims=True)
        qs = _dot(jnp.broadcast_to(rq, (8, dk)).astype(BF16), s.astype(BF16))[0:1, :]
        o = sc * v + qs * gam
        sro_ref[h] = s * gam + _dot_tn(_pad_rows(rk).astype(BF16), jnp.broadcast_to(v, (8, dk)).astype(BF16))
        or_ref[:, h * dk:(h + 1) * dk] = (_head_norm(o, gr_ref[h:h + 1, :]) * (rg * _sigmoid(rg))).astype(or_ref.dtype)
        q = qk_act[:, h * dk:(h + 1) * dk]
        k = qk_act[:, rw + h * dk:rw + (h + 1) * dk] * scale
        v = z_ref[:, sl(6)]
        og = z_ref[:, sl(7)]
        it = gt_ref[:, h:h + 1] + bg_ref[h]
        lf = _log_sigmoid(gt_ref[:, heads + h:heads + h + 1] + bg_ref[heads + h])
        m = cm_ref[:, h:h + 1]
        cst = cc_ref[h]
        nst = cn_ref[h:h + 1, :]
        prior = lf + m
        mt = jnp.maximum(prior, it)
        w = jnp.exp(it - mt)
        wp = jnp.exp(prior - mt)
        qk = jnp.sum(q * k, axis=1, keepdims=True) * w
        qc = _dot(jnp.broadcast_to(q, (8, dk)).astype(BF16), cst.astype(BF16))[0:1, :]
        num = qk * v + qc * wp
        den = qk + jnp.sum(q * nst, axis=1, keepdims=True) * wp
        hh = num / jnp.maximum(jnp.abs(den), jnp.exp(-mt))
        wk = jnp.exp(it - mt)
        wc = jnp.exp(lf + m - mt)
        kw = k * wk
        cco_ref[h] = cst * wc + _dot_tn(_pad_rows(kw).astype(BF16), jnp.broadcast_to(v, (8, dk)).astype(BF16))
        cno_ref[h:h + 1, :] = nst * wc + kw
        cmo_ref[:, h:h + 1] = mt
        om_ref[:, h * dk:(h + 1) * dk] = (_head_norm(hh, gm_ref[h:h + 1, :]) * _sigmoid(og)).astype(om_ref.dtype)


def _mix_sample(z, gates, log_g, b_gate, cos_sin, s_ret, s_conv, s_c, s_n, s_m, conv_w, conv_b, g_ret, g_ml, heads, dk):
    nb = z.shape[0]
    rw = heads * dk
    per_b3 = lambda *tail: pl.BlockSpec((None,) + tail, lambda b, *_: (b,) + (0,) * len(tail))
    whole = lambda a: pl.BlockSpec(a.shape, lambda b, *_: (0,) * a.ndim)
    z3 = z.reshape(nb, 1, 8 * rw)
    g3 = gates.reshape(nb, 1, 2 * heads)
    m3 = s_m.reshape(nb, 1, heads)
    cb2 = conv_b.reshape(1, 2 * rw)
    grid_spec = pltpu.PrefetchScalarGridSpec(
        num_scalar_prefetch=2,
        grid=(nb,),
        in_specs=[per_b3(1, 8 * rw), per_b3(1, 2 * heads), whole(cos_sin),
                  per_b3(heads, dk, dk), per_b3(CONV_W - 1, 2 * rw), per_b3(heads, dk, dk), per_b3(heads, dk),
                  per_b3(1, heads), whole(conv_w), whole(cb2), whole(g_ret), whole(g_ml)],
        out_specs=[per_b3(1, rw), per_b3(1, rw), per_b3(heads, dk, dk), per_b3(CONV_W - 1, 2 * rw),
                   per_b3(heads, dk, dk), per_b3(heads, dk), per_b3(1, heads)],
    )
    o_r, o_m, sr, cb, cc, cn, cm = pl.pallas_call(
        functools.partial(_mix_sample_kernel, heads=heads, dk=dk),
        out_shape=[jax.ShapeDtypeStruct((nb, 1, rw), BF16), jax.ShapeDtypeStruct((nb, 1, rw), BF16),
                   jax.ShapeDtypeStruct(s_ret.shape, F32), jax.ShapeDtypeStruct(s_conv.shape, F32),
                   jax.ShapeDtypeStruct(s_c.shape, F32), jax.ShapeDtypeStruct(s_n.shape, F32),
                   jax.ShapeDtypeStruct((nb, 1, heads), F32)],
        grid_spec=grid_spec,
        compiler_params=_cparams("parallel"),
        name="mix_sample",
    )(log_g, b_gate, z3, g3, cos_sin, s_ret, s_conv, s_c, s_n, m3, conv_w, cb2, g_ret, g_ml)
    return o_r.reshape(nb, rw), o_m.reshape(nb, rw), sr, cb, cc, cn, cm.reshape(nb, heads)


def _out_proj_kernel(a_ref, b_ref, wa_ref, wb_ref, x_ref, eg_ref, eb_ref, g_ref, bb_ref, of_ref, ob_ref, ot_ref, *, tn):
    j = pl.program_id(1)
    cols = pl.ds(pl.multiple_of(j * tn, tn), tn)
    of_ref[:, cols] = _dot(a_ref[...], wa_ref[...]) + _dot(b_ref[...], wb_ref[...])

    @pl.when(j == pl.num_programs(1) - 1)
    def _():
        def rows_step(c, carry):
            rows = pl.ds(pl.multiple_of(c * LANES, LANES), LANES)
            xe = _layer_norm(x_ref[rows, :], eg_ref[...], eb_ref[...])
            x1 = _layer_norm(DEEPNORM_ALPHA * xe + of_ref[rows, :], g_ref[...], bb_ref[...])
            of_ref[rows, :] = x1
            ob_ref[rows, :] = x1.astype(BF16)
            ot_ref[:, rows] = x1.T.astype(BF16)
            return carry

        lax.fori_loop(0, x_ref.shape[0] // LANES, rows_step, 0)


def _out_proj_ln1(o_r, o_m, w_out_b, x_raw, eg, eb, g, b):
    n, d = x_raw.shape
    ka = o_r.shape[1]
    tm = _pick(n, 512, 256, 128)
    tn = 1024
    row = lambda a: a.reshape(1, d)
    vec = pl.BlockSpec((1, d), lambda i, j: (0, 0))
    once = pl.Buffered(1)
    return pl.pallas_call(
        functools.partial(_out_proj_kernel, tn=tn),
        out_shape=[jax.ShapeDtypeStruct((n, d), F32), jax.ShapeDtypeStruct((n, d), BF16),
                   jax.ShapeDtypeStruct((d, n), BF16)],
        grid=(n // tm, d // tn),
        in_specs=[pl.BlockSpec((tm, ka), lambda i, j: (i, 0), pipeline_mode=once),
                  pl.BlockSpec((tm, ka), lambda i, j: (i, 0), pipeline_mode=once),
                  pl.BlockSpec((ka, tn), lambda i, j: (0, j)),
                  pl.BlockSpec((ka, tn), lambda i, j: (1, j)),
                  pl.BlockSpec((tm, d), lambda i, j: (i, 0), pipeline_mode=once),
                  vec, vec, vec, vec],
        out_specs=[pl.BlockSpec((tm, d), lambda i, j: (i, 0), pipeline_mode=once),
                   pl.BlockSpec((tm, d), lambda i, j: (i, 0), pipeline_mode=once),
                   pl.BlockSpec((d, tm), lambda i, j: (0, i), pipeline_mode=once)],
        compiler_params=_cparams("parallel", "arbitrary"),
        name="out_proj_ln1",
    )(o_r, o_m, w_out_b, w_out_b, x_raw, row(eg), row(eb), row(g), row(b))


def _peer_scores_kernel(x_ref, wq_ref, keys_ref, o_ref):
    q = _dot(x_ref[...], wq_ref[...])
    nk, kd = keys_ref.shape[1], keys_ref.shape[2]
    for hp in range(keys_ref.shape[0]):
        qh = q[:, hp * kd:(hp + 1) * kd].astype(BF16)
        o_ref[hp] = _dot_nt(keys_ref[hp], qh)


def _peer_scores(x1b, wq_b, keys_b):
    n, d = x1b.shape
    hp, nk, kd = keys_b.shape
    tb = _pick(n, 512, 256, 128)
    return pl.pallas_call(
        _peer_scores_kernel,
        out_shape=jax.ShapeDtypeStruct((hp, nk, n), F32),
        grid=(n // tb,),
        in_specs=[pl.BlockSpec((tb, d), lambda i: (i, 0)),
                  pl.BlockSpec(wq_b.shape, lambda i: (0, 0), pipeline_mode=pl.Buffered(1)),
                  pl.BlockSpec(keys_b.shape, lambda i: (0, 0, 0))],
        out_specs=pl.BlockSpec((hp, nk, tb), lambda i: (0, 0, i)),
        compiler_params=_cparams("parallel"),
        name="peer_scores",
    )(x1b, wq_b, keys_b)


def _top_ranks(s, k):
    n, tb = s.shape
    idx = lax.broadcasted_iota(I32, (n, tb), 0)
    rank = jnp.full((n, tb), k, I32)
    vals = []
    for r in range(k):
        m = jnp.max(s, axis=0, keepdims=True)
        first = jnp.min(jnp.where(s == m, idx, n), axis=0, keepdims=True)
        hit = idx == first
        rank = jnp.where(hit, r, rank)
        s = jnp.where(hit, -jnp.inf, s)
        vals.append(m)
    return jnp.concatenate(vals, axis=0), rank


def _route_head_exact(s1, s2, topk):
    tb = s1.shape[1]
    bit = jnp.left_shift(1, lax.broadcasted_iota(I32, (topk, tb), 0))
    a, r1 = _top_ranks(s1, topk)
    b, r2 = _top_ranks(s2, topk)
    cand = jnp.concatenate([a[r:r + 1, :] + b for r in range(topk)], axis=0)
    tv, rc = _top_ranks(cand, topk)
    z = jnp.sum(jnp.exp(tv - tv[0:1, :]), axis=0, keepdims=True)
    rb = jnp.zeros(s1.shape, I32)
    for r in range(topk):
        sel = rc[r * topk:(r + 1) * topk, :] < topk
        rowbits = jnp.sum(jnp.where(sel, bit, 0), axis=0, keepdims=True)
        rb = jnp.where(r1 == r, rowbits, rb)
    e1 = jnp.where(r1 < topk, jnp.exp(s1 - a[0:1, :]) / z, 0.0)
    e2 = jnp.where(r2 < topk, jnp.exp(s2 - b[0:1, :]), 0.0)
    b2 = jnp.where(r2 < topk, jnp.left_shift(1, jnp.minimum(r2, topk - 1)), 0)
    return e1, rb, e2, b2


def _top_values(s, k, on_hit):
    vals = []
    for r in range(k):
        m = jnp.max(s, axis=0, keepdims=True)
        hit = s == m
        on_hit(r, hit)
        s = jnp.where(hit, -jnp.inf, s)
        vals.append(m)
    return vals


def _route_head_fast(s1, s2, topk):
    n, tb = s1.shape
    sub = 8
    assert topk == 2 * sub
    st = dict(r1=jnp.full((n, tb), topk, I32), b2=jnp.zeros((n, tb), I32))

    def hit1(r, hit):
        st["r1"] = jnp.where(hit, r, st["r1"])

    def hit2(r, hit):
        st["b2"] = jnp.where(hit, 1 << r, st["b2"])

    a = _top_values(s1, topk, hit1)
    b = _top_values(s2, topk, hit2)
    r1, b2 = st["r1"], st["b2"]
    a_arr = jnp.concatenate(a, axis=0)
    b_arr = jnp.concatenate(b, axis=0)
    row = lax.broadcasted_iota(I32, (sub, tb), 0)
    slabs, cols = [], []
    for r2 in range(sub):
        lim = topk // (r2 + 1)
        for r1s in range(0, lim, sub):
            slab = a_arr[r1s:r1s + sub, :] + b_arr[r2:r2 + 1, :]
            if lim - r1s < sub:
                slab = jnp.where(row < lim - r1s, slab, -jnp.inf)
            slabs.append(slab)
            cols.append((r1s, r2))
    slabs.append(a_arr[0:1, :] + b_arr[sub:topk, :])
    cand = jnp.concatenate(slabs, axis=0)
    st["sel"] = jnp.zeros(cand.shape, I32)

    def hit3(r, hit):
        st["sel"] = jnp.where(hit, 1, st["sel"])

    tv = _top_values(cand, topk, hit3)
    sel = st["sel"]
    rowbits = [jnp.zeros((sub, tb), I32) for _ in range(topk // sub)]
    for i, (r1s, r2) in enumerate(cols):
        rowbits[r1s // sub] = rowbits[r1s // sub] + jnp.left_shift(sel[i * sub:(i + 1) * sub, :], r2)
    last = jnp.sum(jnp.left_shift(sel[len(cols) * sub:, :], row + sub), axis=0, keepdims=True)
    rowbits[0] = rowbits[0] + jnp.where(row == 0, last, 0)
    count = lambda m: jnp.sum(m.astype(I32), axis=0, keepdims=True)
    ok = (count(r1 < topk) == topk) & (count(b2 != 0) == topk) & (count(sel) == topk)
    z = jnp.ones_like(tv[0])
    for r in range(1, topk):
        z = z + jnp.exp(tv[r] - tv[0])
    rb = jnp.zeros((n, tb), I32)
    for r in range(topk):
        rb = jnp.where(r1 == r, rowbits[r // sub][r % sub:r % sub + 1, :], rb)
    e1 = jnp.where(r1 < topk, jnp.exp(s1 - a[0]) / z, 0.0)
    e2 = jnp.where(b2 != 0, jnp.exp(s2 - b[0]), 0.0)
    return e1, rb, e2, b2, ok


def _peer_route_kernel(s_ref, e1_ref, rb_ref, e2_ref, b2_ref, *, topk):
    heads = e1_ref.shape[0]
    tb = s_ref.shape[2]
    bad = jnp.zeros((1, tb), I32)
    for h in range(heads):
        e1, rb, e2, b2, ok = _route_head_fast(s_ref[2 * h], s_ref[2 * h + 1], topk)
        e1_ref[h], rb_ref[h], e2_ref[h], b2_ref[h] = e1, rb, e2, b2
        bad = jnp.where(ok, bad, 1)

    @pl.when(jnp.max(bad) > 0)
    def _():
        def redo(h, carry):
            e1, rb, e2, b2 = _route_head_exact(s_ref[2 * h], s_ref[2 * h + 1], topk)
            e1_ref[h], rb_ref[h], e2_ref[h], b2_ref[h] = e1, rb, e2, b2
            return carry

        lax.fori_loop(0, heads, redo, 0)


def _peer_route(scores, heads):
    hp, nk, n = scores.shape
    tb = LANES
    spec = pl.BlockSpec((heads, nk, tb), lambda i: (0, 0, i))
    return pl.pallas_call(
        functools.partial(_peer_route_kernel, topk=PEER_TOPK),
        out_shape=[jax.ShapeDtypeStruct((heads, nk, n), F32), jax.ShapeDtypeStruct((heads, nk, n), I32),
                   jax.ShapeDtypeStruct((heads, nk, n), F32), jax.ShapeDtypeStruct((heads, nk, n), I32)],
        grid=(n // tb,),
        in_specs=[pl.BlockSpec((hp, nk, tb), lambda i: (0, 0, i))],
        out_specs=[spec, spec, spec, spec],
        compiler_params=_cparams("parallel"),
        name="peer_route",
    )(scores)


def _gelu_tanh(x):
    c = math.sqrt(2.0 / math.pi)
    return x * (0.5 * (1.0 + jnp.tanh(c * (x + 0.044715 * (x * x * x)))))


def _peer_dense_kernel(xt_ref, u_ref, vt_ref, e1_ref, rb_ref, e2_ref, b2_ref, y_ref, acc_ref, *, ti, sub):
    j = pl.program_id(1)
    heads, nk, tb = e2_ref.shape

    @pl.when(j == 0)
    def _():
        acc_ref[...] = jnp.zeros_like(acc_ref)

    tiles = [slice(s * sub * nk, (s + 1) * sub * nk) for s in range(ti // sub)]
    acts = [_dot(u_ref[rows, :], xt_ref[...]) for rows in tiles]
    for s, rows in enumerate(tiles):
        act = _gelu_tanh(acts[s])
        parts = []
        for t in range(sub):
            i1 = pl.ds(j * ti + s * sub + t, 1)
            g = jnp.zeros((nk, tb), F32)
            for h in range(heads):
                hit = (rb_ref[h, i1, :] & b2_ref[h]) != 0
                g = g + jnp.where(hit, e1_ref[h, i1, :] * e2_ref[h], 0.0)
            parts.append((act[t * nk:(t + 1) * nk, :] * g).astype(BF16))
        acc_ref[...] += _dot(vt_ref[:, rows], jnp.concatenate(parts, axis=0))

    @pl.when(j == pl.num_programs(1) - 1)
    def _():
        y_ref[...] = acc_ref[...].T


def _peer_dense(x1t, u_b, vt_b, e1, rb, e2, b2):
    d, n = x1t.shape
    ne = u_b.shape[0]
    heads, nk, _ = e1.shape
    tb = _pick(n, 512, 256, 128)
    te = 512
    ti = te // nk
    sub = 2
    once = pl.Buffered(1)
    rspec = pl.BlockSpec((heads, nk, tb), lambda i, j: (0, 0, i), pipeline_mode=once)
    return pl.pallas_call(
        functools.partial(_peer_dense_kernel, ti=ti, sub=sub),
        out_shape=jax.ShapeDtypeStruct((n, d), F32),
        grid=(n // tb, ne // te),
        in_specs=[pl.BlockSpec((d, tb), lambda i, j: (0, i), pipeline_mode=once),
                  pl.BlockSpec((te, d), lambda i, j: (j, 0)),
                  pl.BlockSpec((d, te), lambda i, j: (0, j)),
                  rspec, rspec, rspec, rspec],
        out_specs=pl.BlockSpec((tb, d), lambda i, j: (i, 0), pipeline_mode=once),
        scratch_shapes=[pltpu.VMEM((d, tb), F32)],
        compiler_params=_cparams("parallel", "arbitrary"),
        name="peer_dense",
    )(x1t, u_b, vt_b, e1, rb, e2, b2)


def _cast_kernel(x_ref, o_ref):
    o_ref[...] = x_ref[...].astype(BF16)


def _cast_t_kernel(x_ref, o_ref):
    o_ref[...] = x_ref[...].T.astype(BF16)


def _cast_rows(x, transpose):
    r, d = x.shape
    tr = _pick(r, 512, 256, 128)
    if transpose:
        kern, oshape, ospec = _cast_t_kernel, (d, r), pl.BlockSpec((d, tr), lambda i: (0, i))
    else:
        kern, oshape, ospec = _cast_kernel, (r, d), pl.BlockSpec((tr, d), lambda i: (i, 0))
    return pl.pallas_call(
        kern,
        out_shape=jax.ShapeDtypeStruct(oshape, BF16),
        grid=(r // tr,),
        in_specs=[pl.BlockSpec((tr, d), lambda i: (i, 0))],
        out_specs=ospec,
        compiler_params=_cparams("parallel"),
        name="cast_t" if transpose else "cast",
    )(x)


def _final_kernel(xf_ref, xb_ref, ch_ref, p_ref, wg_ref, wp_ref, g_ref, b_ref, o_ref, *, tn):
    j = pl.program_id(1)
    cols = pl.ds(pl.multiple_of(j * tn, tn), tn)
    gate = _sigmoid(_dot(xb_ref[...], wg_ref[...]))
    proj = _dot(p_ref[...].astype(BF16), wp_ref[...])
    o_ref[:, cols] = DEEPNORM_ALPHA * xf_ref[...] + ch_ref[...] + gate * proj

    @pl.when(j == pl.num_programs(1) - 1)
    def _():
        def rows_step(c, carry):
            rows = pl.ds(pl.multiple_of(c * LANES, LANES), LANES)
            o_ref[rows, :] = _layer_norm(o_ref[rows, :], g_ref[...], b_ref[...])
            return carry

        lax.fori_loop(0, o_ref.shape[0] // LANES, rows_step, 0)


def _final(x1f, x1b, ch, p, wg_b, wp_b, g, b):
    n, d = x1f.shape
    pd = p.shape[1]
    tm = _pick(n, 512, 256, 128)
    tn = 1024
    vec = pl.BlockSpec((1, d), lambda i, j: (0, 0))
    rows = lambda w: pl.BlockSpec((tm, w), lambda i, j: (i, 0), pipeline_mode=pl.Buffered(1))
    tile = pl.BlockSpec((tm, tn), lambda i, j: (i, j))
    return pl.pallas_call(
        functools.partial(_final_kernel, tn=tn),
        out_shape=jax.ShapeDtypeStruct((n, d), F32),
        grid=(n // tm, d // tn),
        in_specs=[tile, rows(d), tile, rows(pd),
                  pl.BlockSpec((d, tn), lambda i, j: (0, j)),
                  pl.BlockSpec((pd, tn), lambda i, j: (0, j)),
                  vec, vec],
        out_specs=rows(d),
        compiler_params=_cparams("parallel", "arbitrary"),
        name="ple_ln2",
    )(x1f, x1b, ch, p, wg_b, wp_b, g.reshape(1, d), b.reshape(1, d))


def _rope_tables(pos, half):
    inv = ROPE_BASE ** (-jnp.arange(half, dtype=F32) / half)
    ang = pos.astype(F32)[:, None] * inv[None]
    return jnp.cos(ang), jnp.sin(ang)


def _post_mixer(o_r, o_m, x_raw, p, wts):
    x1f, x1b, x1t = _out_proj_ln1(o_r, o_m, wts["w_out"], x_raw, wts["ln_emb_g"], wts["ln_emb_b"], wts["ln1_g"],
                                  wts["ln1_b"])
    heads = wts["keys"].shape[0] // 2
    scores = _peer_scores(x1b, wts["w_q"], wts["keys"])
    e1, rb, e2, b2 = _peer_route(scores, heads)
    ch = _peer_dense(x1t, wts["u"], wts["vt"], e1, rb, e2, b2)
    return _final(x1f, x1b, ch, p, wts["w_gate"], wts["w_proj"], wts["ln2_g"], wts["ln2_b"])


def kernel(x_prompt, x_sample, state_ret, state_conv, state_mlstm_c, state_mlstm_n, state_mlstm_m, p_prompt, p_sample,
           ln_emb_g, ln_emb_b, w_in, b_gate, conv_w, conv_b, g_ret_norm, g_ml_norm, w_out, ln1_g, ln1_b,
           w_peer_q, peer_sub_keys, peer_u, peer_v, w_ple_gate, w_ple_proj, ln2_g, ln2_b):
    bsz, t, d = x_prompt.shape
    nb = x_sample.shape[0]
    _, _, heads, dk, _ = state_ret.shape
    rw = heads * dk
    n_main = 8 * rw
    assert x_sample.shape[1] == 1 and w_in.shape[0] == DEPTH and t % CHUNK == 0

    w_in0 = w_in[0]
    wg_t = w_in0[:, n_main:].T.astype(BF16)
    log_g = jnp.log1p(-(2.0 ** (-5.0 - jnp.arange(heads, dtype=F32))))
    keys = peer_sub_keys[0]
    wts = dict(
        ln_emb_g=ln_emb_g, ln_emb_b=ln_emb_b, ln1_g=ln1_g[0], ln1_b=ln1_b[0], ln2_g=ln2_g[0], ln2_b=ln2_b[0],
        w_out=w_out[0].astype(BF16), w_q=w_peer_q[0].astype(BF16),
        keys=keys.reshape(keys.shape[0] * 2, keys.shape[2], keys.shape[3]).astype(BF16),
        u=_cast_rows(peer_u[0], False), vt=_cast_rows(peer_v[0], True),
        w_gate=w_ple_gate[0].astype(BF16), w_proj=w_ple_proj[0].astype(BF16),
    )

    xp = x_prompt.reshape(bsz * t, d)
    xn = _ln_cast(xp, ln_emb_g, ln_emb_b)
    z = _in_proj(xn, w_in0, n_main)
    gates_t = _gate_rows(xn, wg_t)
    cos, sin = _rope_tables(jnp.arange(t), dk // 2)
    o_r, ret_p = _ret_prompt(z, log_g, cos, sin, g_ret_norm[0], bsz, t, heads, dk, 0)
    o_m, c_p, n_p, m_p = _mlstm_prompt(z, gates_t, b_gate[0], conv_w[0], conv_b[0].reshape(1, 2 * rw), g_ml_norm[0],
                                       bsz, t, heads, dk, 4 * rw)
    conv_p = z.reshape(bsz, t, n_main)[:, t - (CONV_W - 1):, 4 * rw:6 * rw]
    y_prompt = _post_mixer(o_r, o_m, xp, p_prompt[0].reshape(bsz * t, -1), wts).reshape(bsz, t, d)

    xs = x_sample.reshape(nb, d)
    xns = _ln_cast(xs, ln_emb_g, ln_emb_b)
    zs = _in_proj(xns, w_in0, n_main)
    gates_s = _gate_rows(xns, wg_t).T
    cs, sn = _rope_tables(jnp.full((1,), PAST_LEN), dk // 2)
    o_rs, o_ms, ret_s, conv_s, c_s, n_s, m_s = _mix_sample(
        zs, gates_s, log_g, b_gate[0], jnp.concatenate([cs, sn], axis=0), state_ret[0], state_conv[0],
        state_mlstm_c[0], state_mlstm_n[0], state_mlstm_m[0], conv_w[0], conv_b[0], g_ret_norm[0], g_ml_norm[0], heads, dk)
    y_sample = _post_mixer(o_rs, o_ms, xs, p_sample[0].reshape(nb, -1), wts).reshape(nb, 1, d)

    lead = lambda a: a[None]
    return (y_prompt, y_sample, lead(ret_p), lead(conv_p), lead(c_p), lead(n_p), lead(m_p),
            lead(ret_s), lead(conv_s), lead(c_s), lead(n_s), lead(m_s))
```

```python
import functools
import math

import jax
import jax.numpy as jnp
from jax import lax
from jax.experimental import pallas as pl
from jax.experimental.pallas import tpu as pltpu

F32 = jnp.float32
BF16 = jnp.bfloat16
I32 = jnp.int32

LN_EPS = 1e-5
DEPTH = 1
DEEPNORM_ALPHA = (2.0 * DEPTH) ** 0.25
CHUNK = 128
ROPE_BASE = 10000.0
PAST_LEN = 16384
PEER_TOPK = 16
CONV_W = 4

V7X_VMEM_LIMIT_BYTES = 56 * 1024 * 1024
LANES = 128


def _cparams(*sem):
    return pltpu.CompilerParams(dimension_semantics=sem, vmem_limit_bytes=V7X_VMEM_LIMIT_BYTES)


def _dot(a, b):
    return jnp.dot(a, b, preferred_element_type=F32)


def _dot_nt(a, b):
    return lax.dot_general(a, b, (((1,), (1,)), ((), ())), preferred_element_type=F32)


def _dot_tn(a, b):
    return lax.dot_general(a, b, (((0,), (0,)), ((), ())), preferred_element_type=F32)


def _sigmoid(x):
    return 1.0 / (1.0 + jnp.exp(-x))


def _layer_norm(x, g, b):
    mu = jnp.mean(x, axis=-1, keepdims=True)
    xc = x - mu
    var = jnp.mean(xc * xc, axis=-1, keepdims=True)
    return xc * lax.rsqrt(var + LN_EPS) * g + b


def _head_norm(x, g):
    mu = jnp.mean(x, axis=-1, keepdims=True)
    xc = x - mu
    var = jnp.mean(xc * xc, axis=-1, keepdims=True)
    return xc * lax.rsqrt(var + LN_EPS) * g


def _pick(n, *cands):
    for c in cands:
        if n % c == 0:
            return c
    return n


def _ln_cast_kernel(x_ref, g_ref, b_ref, o_ref):
    o_ref[...] = _layer_norm(x_ref[...], g_ref[...], b_ref[...]).astype(BF16)


def _ln_cast(x, g, b):
    n, d = x.shape
    tm = _pick(n, 256, 128)
    return pl.pallas_call(
        _ln_cast_kernel,
        out_shape=jax.ShapeDtypeStruct((n, d), BF16),
        grid=(n // tm,),
        in_specs=[pl.BlockSpec((tm, d), lambda i: (i, 0)),
                  pl.BlockSpec((1, d), lambda i: (0, 0)),
                  pl.BlockSpec((1, d), lambda i: (0, 0))],
        out_specs=pl.BlockSpec((tm, d), lambda i: (i, 0)),
        compiler_params=_cparams("parallel"),
        name="ln_cast",
    )(x, g.reshape(1, d), b.reshape(1, d))


def _in_proj_kernel(x_ref, w_ref, o_ref, wb_ref):
    @pl.when(pl.program_id(1) == 0)
    def _():
        wb_ref[...] = w_ref[...].astype(BF16)

    o_ref[...] = _dot(x_ref[...], wb_ref[...])


def _in_proj(xn, w_in, n_main):
    n, d = xn.shape
    tm = _pick(n, 1024, 512, 256, 128)
    tn = 512
    return pl.pallas_call(
        _in_proj_kernel,
        out_shape=jax.ShapeDtypeStruct((n, n_main), F32),
        grid=(n_main // tn, n // tm),
        in_specs=[pl.BlockSpec((tm, d), lambda j, i: (i, 0)),
                  pl.BlockSpec((None, d, tn), lambda j, i: (0, 0, j))],
        out_specs=pl.BlockSpec((tm, tn), lambda j, i: (i, j)),
        scratch_shapes=[pltpu.VMEM((d, tn), BF16)],
        compiler_params=_cparams("parallel", "arbitrary"),
        name="in_proj",
    )(xn, w_in)


def _gate_rows_kernel(w_ref, x_ref, o_ref):
    o_ref[...] = _dot_nt(w_ref[...], x_ref[...])


def _gate_rows(xn, wg_t):
    n, d = xn.shape
    g = wg_t.shape[0]
    tb = _pick(n, 1024, 512, 256, 128)
    return pl.pallas_call(
        _gate_rows_kernel,
        out_shape=jax.ShapeDtypeStruct((g, n), F32),
        grid=(n // tb,),
        in_specs=[pl.BlockSpec((g, d), lambda i: (0, 0)),
                  pl.BlockSpec((tb, d), lambda i: (i, 0))],
        out_specs=pl.BlockSpec((g, tb), lambda i: (0, i)),
        compiler_params=_cparams("parallel"),
        name="gate_rows",
    )(wg_t, xn)


def _rope(x, cos, sin):
    half = x.shape[-1] // 2
    x1, x2 = x[:, :half], x[:, half:]
    return jnp.concatenate([x1 * cos - x2 * sin, x1 * sin + x2 * cos], axis=-1)


def _log_sigmoid(x):
    return jnp.minimum(x, 0.0) - jnp.log1p(jnp.exp(-jnp.abs(x)))


def _row_to_col(row, eye):
    return jnp.sum(jnp.where(eye, row, 0.0), axis=1, keepdims=True)


HEAD_GROUP = 2


def _ret_prompt_kernel(lg_ref, q_ref, k_ref, v_ref, g_ref, cos_ref, sin_ref, gn_ref, o_ref, s_ref, *, chunk, dk):
    grp = pl.program_id(1)
    L = chunk
    t = q_ref.shape[0]
    hg = q_ref.shape[1] // dk
    ii = lax.broadcasted_iota(I32, (L, L), 0)
    jj = lax.broadcasted_iota(I32, (L, L), 1)
    causal = ii >= jj
    diff = jnp.where(causal, (ii - jj).astype(F32), 0.0)
    idx = lax.broadcasted_iota(I32, (L, 1), 0).astype(F32)
    scale = dk ** -0.5
    s_ref[...] = jnp.zeros_like(s_ref)
    per_head = []
    for u in range(hg):
        lg = lg_ref[grp * hg + u]
        per_head.append(dict(
            decay_in=jnp.where(causal, jnp.exp(lg * diff), 0.0),
            decay_q=jnp.exp(lg * (idx + 1.0)),
            decay_k=jnp.exp(lg * (float(L) - 1.0 - idx)),
            decay_c=jnp.exp(lg * jnp.full((1, 1), float(L), F32)),
            gn=gn_ref[pl.ds(grp * hg + u, 1), :]))

    def body(c, carry):
        rows = pl.ds(pl.multiple_of(c * L, L), L)
        cos, sin = cos_ref[rows, :], sin_ref[rows, :]
        for u, hd in enumerate(per_head):
            cols = slice(u * dk, (u + 1) * dk)
            rq = _rope(q_ref[rows, cols], cos, sin)
            rk = _rope(k_ref[rows, cols], cos, sin) * scale
            vb = v_ref[rows, cols].astype(BF16)
            rqb = rq.astype(BF16)
            s = s_ref[u]
            sc = _dot_nt(rqb, rk.astype(BF16)) * hd["decay_in"]
            o = _dot(sc.astype(BF16), vb) + _dot(rqb, s.astype(BF16)) * hd["decay_q"]
            s_ref[u] = s * hd["decay_c"] + _dot_tn((rk * hd["decay_k"]).astype(BF16), vb)
            g = g_ref[rows, cols]
            o_ref[rows, cols] = (_head_norm(o, hd["gn"]) * (g * _sigmoid(g))).astype(o_ref.dtype)
        return carry

    lax.fori_loop(0, t // L, body, 0)


def _ret_prompt(z, log_g, cos, sin, g_norm, bsz, t, heads, dk, col0):
    hg = HEAD_GROUP if heads % HEAD_GROUP == 0 else 1
    ng = heads // hg
    gb = col0 // (hg * dk)
    zspec = lambda off: pl.BlockSpec((t, hg * dk), lambda b, g, *_: (b, gb + off * ng + g))
    grid_spec = pltpu.PrefetchScalarGridSpec(
        num_scalar_prefetch=1,
        grid=(bsz, ng),
        in_specs=[zspec(0), zspec(1), zspec(2), zspec(3),
                  pl.BlockSpec((t, dk // 2), lambda b, g, *_: (0, 0)),
                  pl.BlockSpec((t, dk // 2), lambda b, g, *_: (0, 0)),
                  pl.BlockSpec((heads, dk), lambda b, g, *_: (0, 0))],
        out_specs=[pl.BlockSpec((t, hg * dk), lambda b, g, *_: (b, g)),
                   pl.BlockSpec((None, hg, dk, dk), lambda b, g, *_: (b, g, 0, 0))],
    )
    return pl.pallas_call(
        functools.partial(_ret_prompt_kernel, chunk=CHUNK, dk=dk),
        out_shape=[jax.ShapeDtypeStruct((bsz * t, heads * dk), BF16),
                   jax.ShapeDtypeStruct((bsz, heads, dk, dk), F32)],
        grid_spec=grid_spec,
        compiler_params=_cparams("parallel", "parallel"),
        name="ret_prompt",
    )(log_g, z, z, z, z, cos, sin, g_norm)


def _mlstm_prompt_kernel(bg_ref, xq_ref, xk_ref, v_ref, og_ref, ig_ref, fg_ref, cwq_ref, cwk_ref, cbq_ref, cbk_ref,
                         gn_ref, o_ref, c_ref, n_ref, m_ref, bt_s, ig_s, *, chunk, heads, dk):
    grp = pl.program_id(1)
    L = chunk
    t = xq_ref.shape[0]
    hg = xq_ref.shape[1] // dk
    nc = t // L
    scale = dk ** -0.5

    lane = lax.broadcasted_iota(I32, (nc, L), 1)
    for u in range(hg):
        h = grp * hg + u
        ig_s[u] = ig_ref[u] + bg_ref[h]
        bt = _log_sigmoid(fg_ref[u] + bg_ref[heads + h])
        s = 1
        while s < L:
            bt = bt + jnp.where(lane >= s, pltpu.roll(bt, s, axis=1), 0.0)
            s *= 2
        bt_s[u] = bt

    ii = lax.broadcasted_iota(I32, (L, L), 0)
    jj = lax.broadcasted_iota(I32, (L, L), 1)
    causal = ii >= jj
    eye = ii == jj
    row = lax.broadcasted_iota(I32, (L, 1), 0)
    gns = [gn_ref[pl.ds(grp * hg + u, 1), :] for u in range(hg)]
    c_ref[...] = jnp.zeros_like(c_ref)
    n_ref[...] = jnp.zeros_like(n_ref)

    def conv_silu(x_ref, w_ref, b_ref, c, rows, cols):
        x = x_ref[rows, cols]
        prev_rows = pl.ds(pl.multiple_of(jnp.maximum(c - 1, 0) * L, L), L)
        xp = jnp.where(c > 0, x_ref[prev_rows, cols], 0.0)
        y = x * w_ref[CONV_W - 1:CONV_W, cols] + b_ref[:, cols]
        for j in range(1, CONV_W):
            xs = jnp.where(row < j, pltpu.roll(xp, j, axis=0), pltpu.roll(x, j, axis=0))
            y = y + xs * w_ref[CONV_W - 1 - j:CONV_W - j, cols]
        return y * _sigmoid(y)

    def head_step(u, c, rows, m):
        cols = slice(u * dk, (u + 1) * dk)
        q = conv_silu(xq_ref, cwq_ref, cbq_ref, c, rows, cols)
        k = conv_silu(xk_ref, cwk_ref, cbk_ref, c, rows, cols) * scale
        vb = v_ref[rows, cols].astype(BF16)
        qb = q.astype(BF16)
        bt_row = bt_s[u, pl.ds(c, 1), :]
        ig_row = ig_s[u, pl.ds(c, 1), :]
        bt_col = _row_to_col(bt_row, eye)
        ig_col = _row_to_col(ig_row, eye)
        dmat = jnp.where(causal, bt_col - bt_row + ig_row, -jnp.inf)
        prior = bt_col + m
        mt = jnp.maximum(prior, jnp.max(dmat, axis=1, keepdims=True))
        w = jnp.exp(dmat - mt)
        wp = jnp.exp(prior - mt)
        qk = _dot_nt(qb, k.astype(BF16)) * w
        cst = c_ref[u]
        nst = n_ref[u]
        num = _dot(qk.astype(BF16), vb) + _dot(qb, cst.astype(BF16)) * wp
        den = jnp.sum(qk, axis=1, keepdims=True) + jnp.sum(q * nst, axis=1, keepdims=True) * wp
        hh = num / jnp.maximum(jnp.abs(den), jnp.exp(-mt))
        bl = bt_row[:, L - 1:L]
        m_new = mt[L - 1:L, :]
        wk = jnp.exp(bl - bt_col + ig_col - m_new)
        wc = jnp.exp(bl + m - m_new)
        kw = k * wk
        c_ref[u] = cst * wc + _dot_tn(kw.astype(BF16), vb)
        n_ref[u] = nst * wc + jnp.sum(kw, axis=0, keepdims=True)
        og = og_ref[rows, cols]
        o_ref[rows, cols] = (_head_norm(hh, gns[u]) * _sigmoid(og)).astype(o_ref.dtype)
        return m_new

    def body(c, ms):
        rows = pl.ds(pl.multiple_of(c * L, L), L)
        return tuple(head_step(u, c, rows, ms[u]) for u in range(hg))

    m_fin = lax.fori_loop(0, nc, body, tuple(jnp.zeros((1, 1), F32) for _ in range(hg)))
    for u in range(hg):
        m_ref[u] = jnp.broadcast_to(m_fin[u], m_ref.shape[1:])


def _mlstm_prompt(z, gates_t, b_gate, conv_w, conv_b, g_norm, bsz, t, heads, dk, col0):
    hg = HEAD_GROUP if heads % HEAD_GROUP == 0 else 1
    ng = heads // hg
    gb = col0 // (hg * dk)
    nc = t // CHUNK
    zspec = lambda off: pl.BlockSpec((t, hg * dk), lambda b, g, *_: (b, gb + off * ng + g))
    gates4 = gates_t.reshape(2 * heads, bsz, nc, CHUNK)
    gspec = lambda off: pl.BlockSpec((hg, None, nc, CHUNK), lambda b, g, *_: (off * ng + g, b, 0, 0))
    wspec = lambda rows, off: pl.BlockSpec((rows, hg * dk), lambda b, g, *_: (0, off * ng + g))
    grid_spec = pltpu.PrefetchScalarGridSpec(
        num_scalar_prefetch=1,
        grid=(bsz, ng),
        in_specs=[zspec(0), zspec(1), zspec(2), zspec(3), gspec(0), gspec(1),
                  wspec(CONV_W, 0), wspec(CONV_W, 1), wspec(1, 0), wspec(1, 1),
                  pl.BlockSpec((heads, dk), lambda b, g, *_: (0, 0))],
        out_specs=[pl.BlockSpec((t, hg * dk), lambda b, g, *_: (b, g)),
                   pl.BlockSpec((None, hg, dk, dk), lambda b, g, *_: (b, g, 0, 0)),
                   pl.BlockSpec((None, hg, 1, dk), lambda b, g, *_: (b, g, 0, 0)),
                   pl.BlockSpec((None, hg, 1, LANES), lambda b, g, *_: (b, g, 0, 0))],
        scratch_shapes=[pltpu.VMEM((hg, nc, CHUNK), F32), pltpu.VMEM((hg, nc, CHUNK), F32)],
    )
    o, c, n, m = pl.pallas_call(
        functools.partial(_mlstm_prompt_kernel, chunk=CHUNK, heads=heads, dk=dk),
        out_shape=[jax.ShapeDtypeStruct((bsz * t, heads * dk), BF16),
                   jax.ShapeDtypeStruct((bsz, heads, dk, dk), F32),
                   jax.ShapeDtypeStruct((bsz, heads, 1, dk), F32),
                   jax.ShapeDtypeStruct((bsz, heads, 1, LANES), F32)],
        grid_spec=grid_spec,
        compiler_params=_cparams("parallel", "parallel"),
        name="mlstm_prompt",
    )(b_gate, z, z, z, z, gates4, gates4, conv_w, conv_w, conv_b, conv_b, g_norm)
    return o, c, n[:, :, 0, :], m[:, :, 0, 0]


def _pad_rows(row, rows=8):
    r = lax.broadcasted_iota(I32, (rows, row.shape[1]), 0)
    return jnp.where(r == 0, row, 0.0)


def _mix_sample_kernel(lg_ref, bg_ref, z_ref, gt_ref, cs_ref, sr_ref, cb_ref, cc_ref, cn_ref, cm_ref, cw_ref, cbias_ref,
                       gr_ref, gm_ref, or_ref, om_ref, sro_ref, cbo_ref, cco_ref, cno_ref, cmo_ref, *, heads, dk):
    rw = heads * dk
    cos, sin = cs_ref[0:1, :], cs_ref[1:2, :]
    scale = dk ** -0.5
    xqk = z_ref[:, 4 * rw:6 * rw]
    buf = cb_ref[...]
    y = xqk * cw_ref[CONV_W - 1:CONV_W, :] + cbias_ref[...]
    for j in range(CONV_W - 1):
        y = y + buf[j:j + 1, :] * cw_ref[j:j + 1, :]
    qk_act = y * _sigmoid(y)
    cbo_ref[0:CONV_W - 2, :] = buf[1:CONV_W - 1, :]
    cbo_ref[CONV_W - 2:CONV_W - 1, :] = xqk

    for h in range(heads):
        sl = lambda g: slice(g * rw + h * dk, g * rw + (h + 1) * dk)
        gam = jnp.exp(jnp.full((1, 1), lg_ref[h], F32))
        rq = _rope(z_ref[:, sl(0)], cos, sin)
        rk = _rope(z_ref[:, sl(1)], cos, sin) * scale
        v = z_ref[:, sl(2)]
        rg = z_ref[:, sl(3)]
        s = sr_ref[h]
        sc = jnp.sum(rq * rk, axis=1, keepdims=True)
        qs = _dot(jnp.broadcast_to(rq, (8, dk)).astype(BF16), s.astype(BF16))[0:1, :]
        o = sc * v + qs * gam
        sro_ref[h] = s * gam + _dot_tn(_pad_rows(rk).astype(BF16), jnp.broadcast_to(v, (8, dk)).astype(BF16))
        or_ref[:, h * dk:(h + 1) * dk] = (_head_norm(o, gr_ref[h:h + 1, :]) * (rg * _sigmoid(rg))).astype(or_ref.dtype)
        q = qk_act[:, h * dk:(h + 1) * dk]
        k = qk_act[:, rw + h * dk:rw + (h + 1) * dk] * scale
        v = z_ref[:, sl(6)]
        og = z_ref[:, sl(7)]
        it = gt_ref[:, h:h + 1] + bg_ref[h]
        lf = _log_sigmoid(gt_ref[:, heads + h:heads + h + 1] + bg_ref[heads + h])
        m = cm_ref[:, h:h + 1]
        cst = cc_ref[h]
        nst = cn_ref[h:h + 1, :]
        prior = lf + m
        mt = jnp.maximum(prior, it)
        w = jnp.exp(it - mt)
        wp = jnp.exp(prior - mt)
        qk = jnp.sum(q * k, axis=1, keepdims=True) * w
        qc = _dot(jnp.broadcast_to(q, (8, dk)).astype(BF16), cst.astype(BF16))[0:1, :]
        num = qk * v + qc * wp
        den = qk + jnp.sum(q * nst, axis=1, keepdims=True) * wp
        hh = num / jnp.maximum(jnp.abs(den), jnp.exp(-mt))
        wk = jnp.exp(it - mt)
        wc = jnp.exp(lf + m - mt)
        kw = k * wk
        cco_ref[h] = cst * wc + _dot_tn(_pad_rows(kw).astype(BF16), jnp.broadcast_to(v, (8, dk)).astype(BF16))
        cno_ref[h:h + 1, :] = nst * wc + kw
        cmo_ref[:, h:h + 1] = mt
        om_ref[:, h * dk:(h + 1) * dk] = (_head_norm(hh, gm_ref[h:h + 1, :]) * _sigmoid(og)).astype(om_ref.dtype)


def _mix_sample(z, gates, log_g, b_gate, cos_sin, s_ret, s_conv, s_c, s_n, s_m, conv_w, conv_b, g_ret, g_ml, heads, dk):
    nb = z.shape[0]
    rw = heads * dk
    per_b3 = lambda *tail: pl.BlockSpec((None,) + tail, lambda b, *_: (b,) + (0,) * len(tail))
    whole = lambda a: pl.BlockSpec(a.shape, lambda b, *_: (0,) * a.ndim)
    z3 = z.reshape(nb, 1, 8 * rw)
    g3 = gates.reshape(nb, 1, 2 * heads)
    m3 = s_m.reshape(nb, 1, heads)
    cb2 = conv_b.reshape(1, 2 * rw)
    grid_spec = pltpu.PrefetchScalarGridSpec(
        num_scalar_prefetch=2,
        grid=(nb,),
        in_specs=[per_b3(1, 8 * rw), per_b3(1, 2 * heads), whole(cos_sin),
                  per_b3(heads, dk, dk), per_b3(CONV_W - 1, 2 * rw), per_b3(heads, dk, dk), per_b3(heads, dk),
                  per_b3(1, heads), whole(conv_w), whole(cb2), whole(g_ret), whole(g_ml)],
        out_specs=[per_b3(1, rw), per_b3(1, rw), per_b3(heads, dk, dk), per_b3(CONV_W - 1, 2 * rw),
                   per_b3(heads, dk, dk), per_b3(heads, dk), per_b3(1, heads)],
    )
    o_r, o_m, sr, cb, cc, cn, cm = pl.pallas_call(
        functools.partial(_mix_sample_kernel, heads=heads, dk=dk),
        out_shape=[jax.ShapeDtypeStruct((nb, 1, rw), BF16), jax.ShapeDtypeStruct((nb, 1, rw), BF16),
                   jax.ShapeDtypeStruct(s_ret.shape, F32), jax.ShapeDtypeStruct(s_conv.shape, F32),
                   jax.ShapeDtypeStruct(s_c.shape, F32), jax.ShapeDtypeStruct(s_n.shape, F32),
                   jax.ShapeDtypeStruct((nb, 1, heads), F32)],
        grid_spec=grid_spec,
        compiler_params=_cparams("parallel"),
        name="mix_sample",
    )(log_g, b_gate, z3, g3, cos_sin, s_ret, s_conv, s_c, s_n, m3, conv_w, cb2, g_ret, g_ml)
    return o_r.reshape(nb, rw), o_m.reshape(nb, rw), sr, cb, cc, cn, cm.reshape(nb, heads)


def _lag_specs(n, ns):
    tm = _pick(n, *(c for c in (512, 256, 128) if c // LANES <= ns))
    mt = n // tm
    nchunks = tm // LANES
    cur = lambda i: jnp.minimum(i, mt - 1)
    chunk = lambda i, j: jnp.maximum(i - 1, 0) * nchunks + jnp.where(i == 0, 0, jnp.minimum(j, nchunks - 1))
    return tm, mt, nchunks, cur, chunk


def _lag_epilogue(j, nchunks, ns, fn):
    if nchunks == ns:
        fn()
    else:
        pl.when(j < nchunks)(fn)


def _lag_run(acc0_ref, acc1_ref, step):
    i, j = pl.program_id(0), pl.program_id(1)

    @pl.when((i == 0) & (j == 0))
    def _():
        acc1_ref[...] = jnp.zeros_like(acc1_ref)

    @pl.when(i % 2 == 0)
    def _():
        step(acc0_ref, acc1_ref)

    @pl.when(i % 2 == 1)
    def _():
        step(acc1_ref, acc0_ref)


def _out_proj_kernel(a_ref, b_ref, wa_ref, wb_ref, x_ref, eg_ref, eb_ref, g_ref, bb_ref, of_ref, ob_ref, ot_ref,
                     acc0_ref, acc1_ref, *, tn, nchunks):
    j = pl.program_id(1)

    def step(fill_ref, done_ref):
        cols = pl.ds(pl.multiple_of(j * tn, tn), tn)
        fill_ref[:, cols] = _dot(a_ref[...], wa_ref[...]) + _dot(b_ref[...], wb_ref[...])

        def norm_chunk():
            rows = pl.ds(pl.multiple_of(j * LANES, LANES), LANES)
            xe = _layer_norm(x_ref[...], eg_ref[...], eb_ref[...])
            x1 = _layer_norm(DEEPNORM_ALPHA * xe + done_ref[rows, :], g_ref[...], bb_ref[...])
            of_ref[...] = x1
            ob_ref[...] = x1.astype(BF16)
            ot_ref[...] = x1.T.astype(BF16)

        _lag_epilogue(j, nchunks, pl.num_programs(1), norm_chunk)

    _lag_run(acc0_ref, acc1_ref, step)


def _out_proj_ln1(o_r, o_m, w_out_b, x_raw, eg, eb, g, b):
    n, d = x_raw.shape
    ka = o_r.shape[1]
    tn = 1024
    ns = d // tn
    tm, mt, nchunks, cur, chunk = _lag_specs(n, ns)
    row = lambda a: a.reshape(1, d)
    vec = pl.BlockSpec((1, d), lambda i, j: (0, 0))
    return pl.pallas_call(
        functools.partial(_out_proj_kernel, tn=tn, nchunks=nchunks),
        out_shape=[jax.ShapeDtypeStruct((n, d), F32), jax.ShapeDtypeStruct((n, d), BF16),
                   jax.ShapeDtypeStruct((d, n), BF16)],
        grid=(mt + 1, ns),
        in_specs=[pl.BlockSpec((tm, ka), lambda i, j: (cur(i), 0), pipeline_mode=pl.Buffered(1)),
                  pl.BlockSpec((tm, ka), lambda i, j: (cur(i), 0), pipeline_mode=pl.Buffered(1)),
                  pl.BlockSpec((ka, tn), lambda i, j: (0, j)),
                  pl.BlockSpec((ka, tn), lambda i, j: (1, j)),
                  pl.BlockSpec((LANES, d), lambda i, j: (chunk(i, j), 0)),
                  vec, vec, vec, vec],
        out_specs=[pl.BlockSpec((LANES, d), lambda i, j: (chunk(i, j), 0)),
                   pl.BlockSpec((LANES, d), lambda i, j: (chunk(i, j), 0)),
                   pl.BlockSpec((d, LANES), lambda i, j: (0, chunk(i, j)))],
        scratch_shapes=[pltpu.VMEM((tm, d), F32), pltpu.VMEM((tm, d), F32)],
        compiler_params=_cparams("arbitrary", "arbitrary"),
        name="out_proj_ln1",
    )(o_r, o_m, w_out_b, w_out_b, x_raw, row(eg), row(eb), row(g), row(b))


def _peer_scores_kernel(x_ref, wq_ref, keys_ref, o_ref):
    q = _dot(x_ref[...], wq_ref[...])
    nk, kd = keys_ref.shape[1], keys_ref.shape[2]
    for hp in range(keys_ref.shape[0]):
        qh = q[:, hp * kd:(hp + 1) * kd].astype(BF16)
        o_ref[hp] = _dot_nt(keys_ref[hp], qh)


def _peer_scores(x1b, wq_b, keys_b):
    n, d = x1b.shape
    hp, nk, kd = keys_b.shape
    tb = _pick(n, 512, 256, 128)
    return pl.pallas_call(
        _peer_scores_kernel,
        out_shape=jax.ShapeDtypeStruct((hp, nk, n), F32),
        grid=(n // tb,),
        in_specs=[pl.BlockSpec((tb, d), lambda i: (i, 0)),
                  pl.BlockSpec(wq_b.shape, lambda i: (0, 0), pipeline_mode=pl.Buffered(1)),
                  pl.BlockSpec(keys_b.shape, lambda i: (0, 0, 0))],
        out_specs=pl.BlockSpec((hp, nk, tb), lambda i: (0, 0, i)),
        compiler_params=_cparams("parallel"),
        name="peer_scores",
    )(x1b, wq_b, keys_b)


def _top_ranks(s, k):
    n, tb = s.shape
    idx = lax.broadcasted_iota(I32, (n, tb), 0)
    rank = jnp.full((n, tb), k, I32)
    vals = []
    for r in range(k):
        m = jnp.max(s, axis=0, keepdims=True)
        first = jnp.min(jnp.where(s == m, idx, n), axis=0, keepdims=True)
        hit = idx == first
        rank = jnp.where(hit, r, rank)
        s = jnp.where(hit, -jnp.inf, s)
        vals.append(m)
    return jnp.concatenate(vals, axis=0), rank


def _route_head_exact(s1, s2, topk):
    a, r1 = _top_ranks(s1, topk)
    b, r2 = _top_ranks(s2, topk)
    cand = jnp.concatenate([a[r:r + 1, :] + b for r in range(topk)], axis=0)
    tv, rc = _top_ranks(cand, topk)
    z = jnp.sum(jnp.exp(tv - tv[0:1, :]), axis=0, keepdims=True)
    c1 = jnp.zeros(s1.shape, I32)
    for r in range(topk):
        sel = rc[r * topk:(r + 1) * topk, :] < topk
        c1 = jnp.where(r1 == r, jnp.sum(sel.astype(I32), axis=0, keepdims=True), c1)
    return jnp.exp(s1 - a[0:1, :]) / z, c1, jnp.exp(s2 - b[0:1, :]), r2


def _top_values(s, k, on_hit):
    vals = []
    for r in range(k):
        m = jnp.max(s, axis=0, keepdims=True)
        hit = s == m
        on_hit(r, hit)
        s = jnp.where(hit, -jnp.inf, s)
        vals.append(m)
    return vals


def _route_head_fast(s1, s2, topk):
    n, tb = s1.shape
    sub = 8
    assert topk == 2 * sub
    st = dict(r1=jnp.full((n, tb), topk, I32), r2=jnp.full((n, tb), topk, I32))

    def hit1(r, hit):
        st["r1"] = jnp.where(hit, r, st["r1"])

    def hit2(r, hit):
        st["r2"] = jnp.where(hit, r, st["r2"])

    a = _top_values(s1, topk, hit1)
    b = _top_values(s2, topk, hit2)
    r1, r2 = st["r1"], st["r2"]
    a_arr = jnp.concatenate(a, axis=0)
    b_arr = jnp.concatenate(b, axis=0)
    row = lax.broadcasted_iota(I32, (sub, tb), 0)
    slabs, cols = [], []
    for q2 in range(sub):
        lim = topk // (q2 + 1)
        for r1s in range(0, lim, sub):
            slab = a_arr[r1s:r1s + sub, :] + b_arr[q2:q2 + 1, :]
            if lim - r1s < sub:
                slab = jnp.where(row < lim - r1s, slab, -jnp.inf)
            slabs.append(slab)
            cols.append((r1s, q2))
    slabs.append(a_arr[0:1, :] + b_arr[sub:topk, :])
    cand = jnp.concatenate(slabs, axis=0)
    st["sel"] = jnp.zeros(cand.shape, I32)

    def hit3(r, hit):
        st["sel"] = jnp.where(hit, 1, st["sel"])

    tv = _top_values(cand, topk, hit3)
    sel = st["sel"]
    count = lambda m: jnp.sum(m.astype(I32), axis=0, keepdims=True)
    rowcount = [jnp.zeros((sub, tb), I32) for _ in range(topk // sub)]
    for i, (r1s, _) in enumerate(cols):
        rowcount[r1s // sub] = rowcount[r1s // sub] + sel[i * sub:(i + 1) * sub, :]
    rowcount[0] = rowcount[0] + jnp.where(row == 0, count(sel[len(cols) * sub:, :]), 0)
    ok = (count(r1 < topk) == topk) & (count(r2 < topk) == topk) & (count(sel) == topk)
    z = jnp.ones_like(tv[0])
    for r in range(1, topk):
        z = z + jnp.exp(tv[r] - tv[0])
    c1 = jnp.zeros((n, tb), I32)
    for r in range(topk):
        c1 = jnp.where(r1 == r, rowcount[r // sub][r % sub:r % sub + 1, :], c1)
    return jnp.exp(s1 - a[0]) / z, c1, jnp.exp(s2 - b[0]), r2, ok


def _peer_route_kernel(s_ref, e1_ref, c1_ref, e2_ref, r2_ref, *, topk):
    heads = e1_ref.shape[0]
    tb = s_ref.shape[2]
    bad = jnp.zeros((1, tb), I32)
    for h in range(heads):
        e1, c1, e2, r2, ok = _route_head_fast(s_ref[2 * h], s_ref[2 * h + 1], topk)
        e1_ref[h], c1_ref[h], e2_ref[h], r2_ref[h] = e1, c1, e2, r2
        bad = jnp.where(ok, bad, 1)

    @pl.when(jnp.max(bad) > 0)
    def _():
        def redo(h, carry):
            e1, c1, e2, r2 = _route_head_exact(s_ref[2 * h], s_ref[2 * h + 1], topk)
            e1_ref[h], c1_ref[h], e2_ref[h], r2_ref[h] = e1, c1, e2, r2
            return carry

        lax.fori_loop(0, heads, redo, 0)


def _peer_route(scores, heads):
    hp, nk, n = scores.shape
    tb = LANES
    spec = pl.BlockSpec((heads, nk, tb), lambda i: (0, 0, i))
    return pl.pallas_call(
        functools.partial(_peer_route_kernel, topk=PEER_TOPK),
        out_shape=[jax.ShapeDtypeStruct((heads, nk, n), F32), jax.ShapeDtypeStruct((heads, nk, n), I32),
                   jax.ShapeDtypeStruct((heads, nk, n), F32), jax.ShapeDtypeStruct((heads, nk, n), I32)],
        grid=(n // tb,),
        in_specs=[pl.BlockSpec((hp, nk, tb), lambda i: (0, 0, i))],
        out_specs=[spec, spec, spec, spec],
        compiler_params=_cparams("parallel"),
        name="peer_route",
    )(scores)


def _gelu_tanh(x):
    c = math.sqrt(2.0 / math.pi)
    return x * (0.5 * (1.0 + jnp.tanh(c * (x + 0.044715 * (x * x * x)))))


def _peer_dense_kernel(xt_ref, u_ref, vt_ref, e1_ref, c1_ref, e2_ref, r2_ref, y_ref, acc_ref, a_ref, *, ti, sub):
    j = pl.program_id(1)
    nj = pl.num_programs(1) - 1
    heads, nk, tb = e2_ref.shape

    @pl.when(j == 0)
    def _():
        acc_ref[...] = jnp.zeros_like(acc_ref)
        a_ref[1] = jnp.zeros(a_ref.shape[1:], a_ref.dtype)

    jc = jnp.minimum(j, nj - 1)
    tiles = [slice(s * sub * nk, (s + 1) * sub * nk) for s in range(ti // sub)]
    acts = [_dot(u_ref[rows, :], xt_ref[...]) for rows in tiles]
    acc_ref[...] += _dot(vt_ref[...], a_ref[(j + 1) % 2])
    for s, rows in enumerate(tiles):
        act = _gelu_tanh(acts[s])
        parts = []
        for t in range(sub):
            i1 = pl.ds(jc * ti + s * sub + t, 1)
            g = jnp.zeros((nk, tb), F32)
            for h in range(heads):
                hit = r2_ref[h] < c1_ref[h, i1, :]
                g = g + jnp.where(hit, e1_ref[h, i1, :] * e2_ref[h], 0.0)
            parts.append((act[t * nk:(t + 1) * nk, :] * g).astype(BF16))
        a_ref[j % 2, rows, :] = jnp.concatenate(parts, axis=0)

    @pl.when(j == nj)
    def _():
        y_ref[...] = acc_ref[...].T


def _peer_dense(x1t, u_b, vt_b, e1, c1, e2, r2):
    d, n = x1t.shape
    ne = u_b.shape[0]
    heads, nk, _ = e1.shape
    tb = _pick(n, 512, 256, 128)
    te = 512
    nj = ne // te
    ti = te // nk
    sub = 2
    once = pl.Buffered(1)
    rspec = pl.BlockSpec((heads, nk, tb), lambda i, j: (0, 0, i), pipeline_mode=once)
    return pl.pallas_call(
        functools.partial(_peer_dense_kernel, ti=ti, sub=sub),
        out_shape=jax.ShapeDtypeStruct((n, d), F32),
        grid=(n // tb, nj + 1),
        in_specs=[pl.BlockSpec((d, tb), lambda i, j: (0, i), pipeline_mode=once),
                  pl.BlockSpec((te, d), lambda i, j: (jnp.minimum(j, nj - 1), 0)),
                  pl.BlockSpec((d, te), lambda i, j: (0, jnp.maximum(j - 1, 0))),
                  rspec, rspec, rspec, rspec],
        out_specs=pl.BlockSpec((tb, d), lambda i, j: (i, 0), pipeline_mode=once),
        scratch_shapes=[pltpu.VMEM((d, tb), F32), pltpu.VMEM((2, te, tb), BF16)],
        compiler_params=_cparams("parallel", "arbitrary"),
        name="peer_dense",
    )(x1t, u_b, vt_b, e1, c1, e2, r2)


def _cast_kernel(x_ref, o_ref):
    o_ref[...] = x_ref[...].astype(BF16)


def _cast_t_kernel(x_ref, o_ref):
    o_ref[...] = x_ref[...].T.astype(BF16)


def _cast_rows(x, transpose):
    r, d = x.shape
    tr = _pick(r, 512, 256, 128)
    if transpose:
        kern, oshape, ospec = _cast_t_kernel, (d, r), pl.BlockSpec((d, tr), lambda i: (0, i))
    else:
        kern, oshape, ospec = _cast_kernel, (r, d), pl.BlockSpec((tr, d), lambda i: (i, 0))
    return pl.pallas_call(
        kern,
        out_shape=jax.ShapeDtypeStruct(oshape, BF16),
        grid=(r // tr,),
        in_specs=[pl.BlockSpec((tr, d), lambda i: (i, 0))],
        out_specs=ospec,
        compiler_params=_cparams("parallel"),
        name="cast_t" if transpose else "cast",
    )(x)


def _final_kernel(xf_ref, xb_ref, ch_ref, p_ref, wg_ref, wp_ref, g_ref, b_ref, o_ref, acc0_ref, acc1_ref, *, tn,
                  nchunks):
    j = pl.program_id(1)

    def step(fill_ref, done_ref):
        cols = pl.ds(pl.multiple_of(j * tn, tn), tn)
        gate = _sigmoid(_dot(xb_ref[...], wg_ref[...]))
        proj = _dot(p_ref[...].astype(BF16), wp_ref[...])
        fill_ref[:, cols] = DEEPNORM_ALPHA * xf_ref[...] + ch_ref[...] + gate * proj

        def norm_chunk():
            rows = pl.ds(pl.multiple_of(j * LANES, LANES), LANES)
            o_ref[...] = _layer_norm(done_ref[rows, :], g_ref[...], b_ref[...])

        _lag_epilogue(j, nchunks, pl.num_programs(1), norm_chunk)

    _lag_run(acc0_ref, acc1_ref, step)


def _final(x1f, x1b, ch, p, wg_b, wp_b, g, b):
    n, d = x1f.shape
    pd = p.shape[1]
    tn = 1024
    ns = d // tn
    tm, mt, nchunks, cur, chunk = _lag_specs(n, ns)
    vec = pl.BlockSpec((1, d), lambda i, j: (0, 0))
    rows = lambda w: pl.BlockSpec((tm, w), lambda i, j: (cur(i), 0), pipeline_mode=pl.Buffered(1))
    tile = pl.BlockSpec((tm, tn), lambda i, j: (cur(i), j))
    return pl.pallas_call(
        functools.partial(_final_kernel, tn=tn, nchunks=nchunks),
        out_shape=jax.ShapeDtypeStruct((n, d), F32),
        grid=(mt + 1, ns),
        in_specs=[tile, rows(d), tile, rows(pd),
                  pl.BlockSpec((d, tn), lambda i, j: (0, j)),
                  pl.BlockSpec((pd, tn), lambda i, j: (0, j)),
                  vec, vec],
        out_specs=pl.BlockSpec((LANES, d), lambda i, j: (chunk(i, j), 0)),
        scratch_shapes=[pltpu.VMEM((tm, d), F32), pltpu.VMEM((tm, d), F32)],
        compiler_params=_cparams("arbitrary", "arbitrary"),
        name="ple_ln2",
    )(x1f, x1b, ch, p, wg_b, wp_b, g.reshape(1, d), b.reshape(1, d))


def _rope_tables(pos, half):
    inv = ROPE_BASE ** (-jnp.arange(half, dtype=F32) / half)
    ang = pos.astype(F32)[:, None] * inv[None]
    return jnp.cos(ang), jnp.sin(ang)


def _post_mixer(o_r, o_m, x_raw, p, wts):
    x1f, x1b, x1t = _out_proj_ln1(o_r, o_m, wts["w_out"], x_raw, wts["ln_emb_g"], wts["ln_emb_b"], wts["ln1_g"],
                                  wts["ln1_b"])
    heads = wts["keys"].shape[0] // 2
    scores = _peer_scores(x1b, wts["w_q"], wts["keys"])
    e1, rb, e2, b2 = _peer_route(scores, heads)
    ch = _peer_dense(x1t, wts["u"], wts["vt"], e1, rb, e2, b2)
    return _final(x1f, x1b, ch, p, wts["w_gate"], wts["w_proj"], wts["ln2_g"], wts["ln2_b"])


def kernel(x_prompt, x_sample, state_ret, state_conv, state_mlstm_c, state_mlstm_n, state_mlstm_m, p_prompt, p_sample,
           ln_emb_g, ln_emb_b, w_in, b_gate, conv_w, conv_b, g_ret_norm, g_ml_norm, w_out, ln1_g, ln1_b,
           w_peer_q, peer_sub_keys, peer_u, peer_v, w_ple_gate, w_ple_proj, ln2_g, ln2_b):
    bsz, t, d = x_prompt.shape
    nb = x_sample.shape[0]
    _, _, heads, dk, _ = state_ret.shape
    rw = heads * dk
    n_main = 8 * rw
    assert x_sample.shape[1] == 1 and w_in.shape[0] == DEPTH and t % CHUNK == 0

    wg_t = w_in[0, :, n_main:].T.astype(BF16)
    log_g = jnp.log1p(-(2.0 ** (-5.0 - jnp.arange(heads, dtype=F32))))
    keys = peer_sub_keys[0]
    wts = dict(
        ln_emb_g=ln_emb_g, ln_emb_b=ln_emb_b, ln1_g=ln1_g[0], ln1_b=ln1_b[0], ln2_g=ln2_g[0], ln2_b=ln2_b[0],
        w_out=w_out[0].astype(BF16), w_q=w_peer_q[0].astype(BF16),
        keys=keys.reshape(keys.shape[0] * 2, keys.shape[2], keys.shape[3]).astype(BF16),
        u=_cast_rows(peer_u[0], False), vt=_cast_rows(peer_v[0], True),
        w_gate=w_ple_gate[0].astype(BF16), w_proj=w_ple_proj[0].astype(BF16),
    )

    xp = x_prompt.reshape(bsz * t, d)
    xn = _ln_cast(xp, ln_emb_g, ln_emb_b)
    z = _in_proj(xn, w_in, n_main)
    gates_t = _gate_rows(xn, wg_t)
    cos, sin = _rope_tables(jnp.arange(t), dk // 2)
    o_r, ret_p = _ret_prompt(z, log_g, cos, sin, g_ret_norm[0], bsz, t, heads, dk, 0)
    o_m, c_p, n_p, m_p = _mlstm_prompt(z, gates_t, b_gate[0], conv_w[0], conv_b[0].reshape(1, 2 * rw), g_ml_norm[0],
                                       bsz, t, heads, dk, 4 * rw)
    conv_p = z.reshape(bsz, t, n_main)[:, t - (CONV_W - 1):, 4 * rw:6 * rw]
    y_prompt = _post_mixer(o_r, o_m, xp, p_prompt[0].reshape(bsz * t, -1), wts).reshape(bsz, t, d)

    xs = x_sample.reshape(nb, d)
    xns = _ln_cast(xs, ln_emb_g, ln_emb_b)
    zs = _in_proj(xns, w_in, n_main)
    gates_s = _gate_rows(xns, wg_t).T
    cs, sn = _rope_tables(jnp.full((1,), PAST_LEN), dk // 2)
    o_rs, o_ms, ret_s, conv_s, c_s, n_s, m_s = _mix_sample(
        zs, gates_s, log_g, b_gate[0], jnp.concatenate([cs, sn], axis=0), state_ret[0], state_conv[0],
        state_mlstm_c[0], state_mlstm_n[0], state_mlstm_m[0], conv_w[0], conv_b[0], g_ret_norm[0], g_ml_norm[0], heads, dk)
    y_sample = _post_mixer(o_rs, o_ms, xs, p_sample[0].reshape(nb, -1), wts).reshape(nb, 1, d)

    lead = lambda a: a[None]
    return (y_prompt, y_sample, lead(ret_p), lead(conv_p), lead(c_p), lead(n_p), lead(m_p),
            lead(ret_s), lead(conv_s), lead(c_s), lead(n_s), lead(m_s))
```

```python
import functools
import math

import jax
import jax.numpy as jnp
from jax import lax
from jax.experimental import pallas as pl
from jax.experimental.pallas import tpu as pltpu

F32 = jnp.float32
BF16 = jnp.bfloat16
I32 = jnp.int32

LN_EPS = 1e-5
DEPTH = 1
DEEPNORM_ALPHA = (2.0 * DEPTH) ** 0.25
CHUNK = 128
ROPE_BASE = 10000.0
PAST_LEN = 16384
PEER_TOPK = 16
PEER_TE = 512
CONV_W = 4

V7X_VMEM_LIMIT_BYTES = 56 * 1024 * 1024
LANES = 128


def _cparams(*sem):
    return pltpu.CompilerParams(dimension_semantics=sem, vmem_limit_bytes=V7X_VMEM_LIMIT_BYTES)


def _dot(a, b):
    return jnp.dot(a, b, preferred_element_type=F32)


def _dot_nt(a, b):
    return lax.dot_general(a, b, (((1,), (1,)), ((), ())), preferred_element_type=F32)


def _dot_tn(a, b):
    return lax.dot_general(a, b, (((0,), (0,)), ((), ())), preferred_element_type=F32)


def _sigmoid(x):
    return 1.0 / (1.0 + jnp.exp(-x))


def _layer_norm(x, g, b):
    mu = jnp.mean(x, axis=-1, keepdims=True)
    xc = x - mu
    var = jnp.mean(xc * xc, axis=-1, keepdims=True)
    return xc * lax.rsqrt(var + LN_EPS) * g + b


def _head_norm(x, g):
    mu = jnp.mean(x, axis=-1, keepdims=True)
    xc = x - mu
    var = jnp.mean(xc * xc, axis=-1, keepdims=True)
    return xc * lax.rsqrt(var + LN_EPS) * g


def _pick(n, *cands):
    for c in cands:
        if n % c == 0:
            return c
    return n


def _ln_cast_kernel(x_ref, g_ref, b_ref, o_ref):
    o_ref[...] = _layer_norm(x_ref[...], g_ref[...], b_ref[...]).astype(BF16)


def _ln_cast(x, g, b):
    n, d = x.shape
    tm = _pick(n, 256, 128)
    return pl.pallas_call(
        _ln_cast_kernel,
        out_shape=jax.ShapeDtypeStruct((n, d), BF16),
        grid=(n // tm,),
        in_specs=[pl.BlockSpec((tm, d), lambda i: (i, 0)),
                  pl.BlockSpec((1, d), lambda i: (0, 0)),
                  pl.BlockSpec((1, d), lambda i: (0, 0))],
        out_specs=pl.BlockSpec((tm, d), lambda i: (i, 0)),
        compiler_params=_cparams("parallel"),
        name="ln_cast",
    )(x, g.reshape(1, d), b.reshape(1, d))


def _in_proj_kernel(x_ref, w_ref, o_ref, wb_ref):
    @pl.when(pl.program_id(1) == 0)
    def _():
        wb_ref[...] = w_ref[...].astype(BF16)

    o_ref[...] = _dot_nt(x_ref[...], wb_ref[...])


def _in_proj(xn, w_in_t, n_main):
    n, d = xn.shape
    tm = _pick(n, 1024, 512, 256, 128)
    tn = 512
    return pl.pallas_call(
        _in_proj_kernel,
        out_shape=jax.ShapeDtypeStruct((n, n_main), F32),
        grid=(n_main // tn, n // tm),
        in_specs=[pl.BlockSpec((tm, d), lambda j, i: (i, 0)),
                  pl.BlockSpec((None, tn, d), lambda j, i: (0, j, 0))],
        out_specs=pl.BlockSpec((tm, tn), lambda j, i: (i, j)),
        scratch_shapes=[pltpu.VMEM((tn, d), BF16)],
        compiler_params=_cparams("parallel", "arbitrary"),
        name="in_proj",
    )(xn, w_in_t)


def _gate_rows_kernel(w_ref, x_ref, o_ref):
    o_ref[...] = _dot_nt(w_ref[...], x_ref[...])


def _gate_rows(xn, wg_t):
    n, d = xn.shape
    g = wg_t.shape[0]
    tb = _pick(n, 1024, 512, 256, 128)
    return pl.pallas_call(
        _gate_rows_kernel,
        out_shape=jax.ShapeDtypeStruct((g, n), F32),
        grid=(n // tb,),
        in_specs=[pl.BlockSpec((g, d), lambda i: (0, 0)),
                  pl.BlockSpec((tb, d), lambda i: (i, 0))],
        out_specs=pl.BlockSpec((g, tb), lambda i: (0, i)),
        compiler_params=_cparams("parallel"),
        name="gate_rows",
    )(wg_t, xn)


def _rope(x, cos, sin):
    half = x.shape[-1] // 2
    x1, x2 = x[:, :half], x[:, half:]
    return jnp.concatenate([x1 * cos - x2 * sin, x1 * sin + x2 * cos], axis=-1)


def _log_sigmoid(x):
    return jnp.minimum(x, 0.0) - jnp.log1p(jnp.exp(-jnp.abs(x)))


def _row_to_col(row, eye):
    return jnp.sum(jnp.where(eye, row, 0.0), axis=1, keepdims=True)


HEAD_GROUP = 2


def _ret_prompt_kernel(lg_ref, q_ref, k_ref, v_ref, g_ref, cos_ref, sin_ref, gn_ref, o_ref, s_ref, *, chunk, dk):
    grp = pl.program_id(1)
    L = chunk
    t = q_ref.shape[0]
    hg = q_ref.shape[1] // dk
    ii = lax.broadcasted_iota(I32, (L, L), 0)
    jj = lax.broadcasted_iota(I32, (L, L), 1)
    causal = ii >= jj
    diff = jnp.where(causal, (ii - jj).astype(F32), 0.0)
    idx = lax.broadcasted_iota(I32, (L, 1), 0).astype(F32)
    scale = dk ** -0.5
    s_ref[...] = jnp.zeros_like(s_ref)
    per_head = []
    for u in range(hg):
        lg = lg_ref[grp * hg + u]
        per_head.append(dict(
            decay_in=jnp.where(causal, jnp.exp(lg * diff), 0.0),
            decay_q=jnp.exp(lg * (idx + 1.0)),
            decay_k=jnp.exp(lg * (float(L) - 1.0 - idx)),
            decay_c=jnp.exp(lg * jnp.full((1, 1), float(L), F32)),
            gn=gn_ref[pl.ds(grp * hg + u, 1), :]))

    def body(c, carry):
        rows = pl.ds(pl.multiple_of(c * L, L), L)
        cos, sin = cos_ref[rows, :], sin_ref[rows, :]
        for u, hd in enumerate(per_head):
            cols = slice(u * dk, (u + 1) * dk)
            rq = _rope(q_ref[rows, cols], cos, sin)
            rk = _rope(k_ref[rows, cols], cos, sin) * scale
            vb = v_ref[rows, cols].astype(BF16)
            rqb = rq.astype(BF16)
            s = s_ref[u]
            sc = _dot_nt(rqb, rk.astype(BF16)) * hd["decay_in"]
            o = _dot(sc.astype(BF16), vb) + _dot(rqb, s.astype(BF16)) * hd["decay_q"]
            s_ref[u] = s * hd["decay_c"] + _dot_tn((rk * hd["decay_k"]).astype(BF16), vb)
            g = g_ref[rows, cols]
            o_ref[rows, cols] = (_head_norm(o, hd["gn"]) * (g * _sigmoid(g))).astype(o_ref.dtype)
        return carry

    lax.fori_loop(0, t // L, body, 0)


def _ret_prompt(z, log_g, cos, sin, g_norm, bsz, t, heads, dk, col0):
    hg = HEAD_GROUP if heads % HEAD_GROUP == 0 else 1
    ng = heads // hg
    gb = col0 // (hg * dk)
    zspec = lambda off: pl.BlockSpec((t, hg * dk), lambda b, g, *_: (b, gb + off * ng + g))
    grid_spec = pltpu.PrefetchScalarGridSpec(
        num_scalar_prefetch=1,
        grid=(bsz, ng),
        in_specs=[zspec(0), zspec(1), zspec(2), zspec(3),
                  pl.BlockSpec((t, dk // 2), lambda b, g, *_: (0, 0)),
                  pl.BlockSpec((t, dk // 2), lambda b, g, *_: (0, 0)),
                  pl.BlockSpec((heads, dk), lambda b, g, *_: (0, 0))],
        out_specs=[pl.BlockSpec((t, hg * dk), lambda b, g, *_: (b, g)),
                   pl.BlockSpec((None, hg, dk, dk), lambda b, g, *_: (b, g, 0, 0))],
    )
    return pl.pallas_call(
        functools.partial(_ret_prompt_kernel, chunk=CHUNK, dk=dk),
        out_shape=[jax.ShapeDtypeStruct((bsz * t, heads * dk), BF16),
                   jax.ShapeDtypeStruct((bsz, heads, dk, dk), F32)],
        grid_spec=grid_spec,
        compiler_params=_cparams("parallel", "parallel"),
        name="ret_prompt",
    )(log_g, z, z, z, z, cos, sin, g_norm)


def _mlstm_prompt_kernel(bg_ref, xq_ref, xk_ref, v_ref, og_ref, ig_ref, fg_ref, cwq_ref, cwk_ref, cbq_ref, cbk_ref,
                         gn_ref, o_ref, c_ref, n_ref, m_ref, bt_s, ig_s, *, chunk, heads, dk):
    grp = pl.program_id(1)
    L = chunk
    t = xq_ref.shape[0]
    hg = xq_ref.shape[1] // dk
    nc = t // L
    scale = dk ** -0.5

    lane = lax.broadcasted_iota(I32, (nc, L), 1)
    for u in range(hg):
        h = grp * hg + u
        ig_s[u] = ig_ref[u] + bg_ref[h]
        bt = _log_sigmoid(fg_ref[u] + bg_ref[heads + h])
        s = 1
        while s < L:
            bt = bt + jnp.where(lane >= s, pltpu.roll(bt, s, axis=1), 0.0)
            s *= 2
        bt_s[u] = bt

    ii = lax.broadcasted_iota(I32, (L, L), 0)
    jj = lax.broadcasted_iota(I32, (L, L), 1)
    causal = ii >= jj
    eye = ii == jj
    row = lax.broadcasted_iota(I32, (L, 1), 0)
    gns = [gn_ref[pl.ds(grp * hg + u, 1), :] for u in range(hg)]
    c_ref[...] = jnp.zeros_like(c_ref)
    n_ref[...] = jnp.zeros_like(n_ref)

    def conv_silu(x_ref, w_ref, b_ref, c, rows, cols):
        x = x_ref[rows, cols]
        prev_rows = pl.ds(pl.multiple_of(jnp.maximum(c - 1, 0) * L, L), L)
        xp = jnp.where(c > 0, x_ref[prev_rows, cols], 0.0)
        y = x * w_ref[CONV_W - 1:CONV_W, cols] + b_ref[:, cols]
        for j in range(1, CONV_W):
            xs = jnp.where(row < j, pltpu.roll(xp, j, axis=0), pltpu.roll(x, j, axis=0))
            y = y + xs * w_ref[CONV_W - 1 - j:CONV_W - j, cols]
        return y * _sigmoid(y)

    def head_step(u, c, rows, m):
        cols = slice(u * dk, (u + 1) * dk)
        q = conv_silu(xq_ref, cwq_ref, cbq_ref, c, rows, cols)
        k = conv_silu(xk_ref, cwk_ref, cbk_ref, c, rows, cols) * scale
        vb = v_ref[rows, cols].astype(BF16)
        qb = q.astype(BF16)
        bt_row = bt_s[u, pl.ds(c, 1), :]
        ig_row = ig_s[u, pl.ds(c, 1), :]
        bt_col = _row_to_col(bt_row, eye)
        ig_col = _row_to_col(ig_row, eye)
        dmat = jnp.where(causal, bt_col - bt_row + ig_row, -jnp.inf)
        prior = bt_col + m
        mt = jnp.maximum(prior, jnp.max(dmat, axis=1, keepdims=True))
        w = jnp.exp(dmat - mt)
        wp = jnp.exp(prior - mt)
        qk = _dot_nt(qb, k.astype(BF16)) * w
        cst = c_ref[u]
        nst = n_ref[u]
        num = _dot(qk.astype(BF16), vb) + _dot(qb, cst.astype(BF16)) * wp
        den = jnp.sum(qk, axis=1, keepdims=True) + jnp.sum(q * nst, axis=1, keepdims=True) * wp
        hh = num / jnp.maximum(jnp.abs(den), jnp.exp(-mt))
        bl = bt_row[:, L - 1:L]
        m_new = mt[L - 1:L, :]
        wk = jnp.exp(bl - bt_col + ig_col - m_new)
        wc = jnp.exp(bl + m - m_new)
        kw = k * wk
        c_ref[u] = cst * wc + _dot_tn(kw.astype(BF16), vb)
        n_ref[u] = nst * wc + jnp.sum(kw, axis=0, keepdims=True)
        og = og_ref[rows, cols]
        o_ref[rows, cols] = (_head_norm(hh, gns[u]) * _sigmoid(og)).astype(o_ref.dtype)
        return m_new

    def body(c, ms):
        rows = pl.ds(pl.multiple_of(c * L, L), L)
        return tuple(head_step(u, c, rows, ms[u]) for u in range(hg))

    m_fin = lax.fori_loop(0, nc, body, tuple(jnp.zeros((1, 1), F32) for _ in range(hg)))
    for u in range(hg):
        m_ref[u] = jnp.broadcast_to(m_fin[u], m_ref.shape[1:])


def _mlstm_prompt(z, gates_t, b_gate, conv_w, conv_b, g_norm, bsz, t, heads, dk, col0):
    hg = HEAD_GROUP if heads % HEAD_GROUP == 0 else 1
    ng = heads // hg
    gb = col0 // (hg * dk)
    nc = t // CHUNK
    zspec = lambda off: pl.BlockSpec((t, hg * dk), lambda b, g, *_: (b, gb + off * ng + g))
    gates4 = gates_t.reshape(2 * heads, bsz, nc, CHUNK)
    gspec = lambda off: pl.BlockSpec((hg, None, nc, CHUNK), lambda b, g, *_: (off * ng + g, b, 0, 0))
    wspec = lambda rows, off: pl.BlockSpec((rows, hg * dk), lambda b, g, *_: (0, off * ng + g))
    grid_spec = pltpu.PrefetchScalarGridSpec(
        num_scalar_prefetch=1,
        grid=(bsz, ng),
        in_specs=[zspec(0), zspec(1), zspec(2), zspec(3), gspec(0), gspec(1),
                  wspec(CONV_W, 0), wspec(CONV_W, 1), wspec(1, 0), wspec(1, 1),
                  pl.BlockSpec((heads, dk), lambda b, g, *_: (0, 0))],
        out_specs=[pl.BlockSpec((t, hg * dk), lambda b, g, *_: (b, g)),
                   pl.BlockSpec((None, hg, dk, dk), lambda b, g, *_: (b, g, 0, 0)),
                   pl.BlockSpec((None, hg, 1, dk), lambda b, g, *_: (b, g, 0, 0)),
                   pl.BlockSpec((None, hg, 1, LANES), lambda b, g, *_: (b, g, 0, 0))],
        scratch_shapes=[pltpu.VMEM((hg, nc, CHUNK), F32), pltpu.VMEM((hg, nc, CHUNK), F32)],
    )
    o, c, n, m = pl.pallas_call(
        functools.partial(_mlstm_prompt_kernel, chunk=CHUNK, heads=heads, dk=dk),
        out_shape=[jax.ShapeDtypeStruct((bsz * t, heads * dk), BF16),
                   jax.ShapeDtypeStruct((bsz, heads, dk, dk), F32),
                   jax.ShapeDtypeStruct((bsz, heads, 1, dk), F32),
                   jax.ShapeDtypeStruct((bsz, heads, 1, LANES), F32)],
        grid_spec=grid_spec,
        compiler_params=_cparams("parallel", "parallel"),
        name="mlstm_prompt",
    )(b_gate, z, z, z, z, gates4, gates4, conv_w, conv_w, conv_b, conv_b, g_norm)
    return o, c, n[:, :, 0, :], m[:, :, 0, 0]


def _pad_rows(row, rows=8):
    r = lax.broadcasted_iota(I32, (rows, row.shape[1]), 0)
    return jnp.where(r == 0, row, 0.0)


def _mix_sample_kernel(lg_ref, bg_ref, z_ref, gt_ref, cs_ref, sr_ref, cb_ref, cc_ref, cn_ref, cm_ref, cw_ref, cbias_ref,
                       gr_ref, gm_ref, or_ref, om_ref, sro_ref, cbo_ref, cco_ref, cno_ref, cmo_ref, *, heads, dk):
    rw = heads * dk
    cos, sin = cs_ref[0:1, :], cs_ref[1:2, :]
    scale = dk ** -0.5
    xqk = z_ref[:, 4 * rw:6 * rw]
    buf = cb_ref[...]
    y = xqk * cw_ref[CONV_W - 1:CONV_W, :] + cbias_ref[...]
    for j in range(CONV_W - 1):
        y = y + buf[j:j + 1, :] * cw_ref[j:j + 1, :]
    qk_act = y * _sigmoid(y)
    cbo_ref[0:CONV_W - 2, :] = buf[1:CONV_W - 1, :]
    cbo_ref[CONV_W - 2:CONV_W - 1, :] = xqk

    for h in range(heads):
        sl = lambda g: slice(g * rw + h * dk, g * rw + (h + 1) * dk)
        gam = jnp.exp(jnp.full((1, 1), lg_ref[h], F32))
        rq = _rope(z_ref[:, sl(0)], cos, sin)
        rk = _rope(z_ref[:, sl(1)], cos, sin) * scale
        v = z_ref[:, sl(2)]
        rg = z_ref[:, sl(3)]
        s = sr_ref[h]
        sc = jnp.sum(rq * rk, axis=1, keepdims=True)
        qs = _dot(jnp.broadcast_to(rq, (8, dk)).astype(BF16), s.astype(BF16))[0:1, :]
        o = sc * v + qs * gam
        sro_ref[h] = s * gam + _dot_tn(_pad_rows(rk).astype(BF16), jnp.broadcast_to(v, (8, dk)).astype(BF16))
        or_ref[:, h * dk:(h + 1) * dk] = (_head_norm(o, gr_ref[h:h + 1, :]) * (rg * _sigmoid(rg))).astype(or_ref.dtype)
        q = qk_act[:, h * dk:(h + 1) * dk]
        k = qk_act[:, rw + h * dk:rw + (h + 1) * dk] * scale
        v = z_ref[:, sl(6)]
        og = z_ref[:, sl(7)]
        it = gt_ref[:, h:h + 1] + bg_ref[h]
        lf = _log_sigmoid(gt_ref[:, heads + h:heads + h + 1] + bg_ref[heads + h])
        m = cm_ref[:, h:h + 1]
        cst = cc_ref[h]
        nst = cn_ref[h:h + 1, :]
        prior = lf + m
        mt = jnp.maximum(prior, it)
        w = jnp.exp(it - mt)
        wp = jnp.exp(prior - mt)
        qk = jnp.sum(q * k, axis=1, keepdims=True) * w
        qc = _dot(jnp.broadcast_to(q, (8, dk)).astype(BF16), cst.astype(BF16))[0:1, :]
        num = qk * v + qc * wp
        den = qk + jnp.sum(q * nst, axis=1, keepdims=True) * wp
        hh = num / jnp.maximum(jnp.abs(den), jnp.exp(-mt))
        wk = jnp.exp(it - mt)
        wc = jnp.exp(lf + m - mt)
        kw = k * wk
        cco_ref[h] = cst * wc + _dot_tn(_pad_rows(kw).astype(BF16), jnp.broadcast_to(v, (8, dk)).astype(BF16))
        cno_ref[h:h + 1, :] = nst * wc + kw
        cmo_ref[:, h:h + 1] = mt
        om_ref[:, h * dk:(h + 1) * dk] = (_head_norm(hh, gm_ref[h:h + 1, :]) * _sigmoid(og)).astype(om_ref.dtype)


def _mix_sample(z, gates, log_g, b_gate, cos_sin, s_ret, s_conv, s_c, s_n, s_m, conv_w, conv_b, g_ret, g_ml, heads, dk):
    nb = z.shape[0]
    rw = heads * dk
    per_b3 = lambda *tail: pl.BlockSpec((None,) + tail, lambda b, *_: (b,) + (0,) * len(tail))
    whole = lambda a: pl.BlockSpec(a.shape, lambda b, *_: (0,) * a.ndim)
    z3 = z.reshape(nb, 1, 8 * rw)
    g3 = gates.reshape(nb, 1, 2 * heads)
    m3 = s_m.reshape(nb, 1, heads)
    cb2 = conv_b.reshape(1, 2 * rw)
    grid_spec = pltpu.PrefetchScalarGridSpec(
        num_scalar_prefetch=2,
        grid=(nb,),
        in_specs=[per_b3(1, 8 * rw), per_b3(1, 2 * heads), whole(cos_sin),
                  per_b3(heads, dk, dk), per_b3(CONV_W - 1, 2 * rw), per_b3(heads, dk, dk), per_b3(heads, dk),
                  per_b3(1, heads), whole(conv_w), whole(cb2), whole(g_ret), whole(g_ml)],
        out_specs=[per_b3(1, rw), per_b3(1, rw), per_b3(heads, dk, dk), per_b3(CONV_W - 1, 2 * rw),
                   per_b3(heads, dk, dk), per_b3(heads, dk), per_b3(1, heads)],
    )
    o_r, o_m, sr, cb, cc, cn, cm = pl.pallas_call(
        functools.partial(_mix_sample_kernel, heads=heads, dk=dk),
        out_shape=[jax.ShapeDtypeStruct((nb, 1, rw), BF16), jax.ShapeDtypeStruct((nb, 1, rw), BF16),
                   jax.ShapeDtypeStruct(s_ret.shape, F32), jax.ShapeDtypeStruct(s_conv.shape, F32),
                   jax.ShapeDtypeStruct(s_c.shape, F32), jax.ShapeDtypeStruct(s_n.shape, F32),
                   jax.ShapeDtypeStruct((nb, 1, heads), F32)],
        grid_spec=grid_spec,
        compiler_params=_cparams("parallel"),
        name="mix_sample",
    )(log_g, b_gate, z3, g3, cos_sin, s_ret, s_conv, s_c, s_n, m3, conv_w, cb2, g_ret, g_ml)
    return o_r.reshape(nb, rw), o_m.reshape(nb, rw), sr, cb, cc, cn, cm.reshape(nb, heads)


def _lag_specs(n, ns):
    tm = _pick(n, *(c for c in (512, 256, 128) if c // LANES <= ns))
    mt = n // tm
    nchunks = tm // LANES
    cur = lambda i: jnp.minimum(i, mt - 1)
    chunk = lambda i, j: jnp.maximum(i - 1, 0) * nchunks + jnp.where(i == 0, 0, jnp.minimum(j, nchunks - 1))
    return tm, mt, nchunks, cur, chunk


def _lag_epilogue(j, nchunks, ns, fn):
    if nchunks == ns:
        fn()
    else:
        pl.when(j < nchunks)(fn)


def _lag_run(acc0_ref, acc1_ref, step):
    i, j = pl.program_id(0), pl.program_id(1)

    @pl.when((i == 0) & (j == 0))
    def _():
        acc1_ref[...] = jnp.zeros_like(acc1_ref)

    @pl.when(i % 2 == 0)
    def _():
        step(acc0_ref, acc1_ref)

    @pl.when(i % 2 == 1)
    def _():
        step(acc1_ref, acc0_ref)


def _out_proj_kernel(a_ref, b_ref, wa_ref, wb_ref, x_ref, eg_ref, eb_ref, g_ref, bb_ref, of_ref, ob_ref,
                     acc0_ref, acc1_ref, *, tn, nchunks):
    j = pl.program_id(1)

    def step(fill_ref, done_ref):
        cols = pl.ds(pl.multiple_of(j * tn, tn), tn)
        fill_ref[:, cols] = _dot(a_ref[...], wa_ref[...]) + _dot(b_ref[...], wb_ref[...])

        def norm_chunk():
            rows = pl.ds(pl.multiple_of(j * LANES, LANES), LANES)
            xe = _layer_norm(x_ref[...], eg_ref[...], eb_ref[...])
            x1 = _layer_norm(DEEPNORM_ALPHA * xe + done_ref[rows, :], g_ref[...], bb_ref[...])
            of_ref[...] = x1
            ob_ref[...] = x1.astype(BF16)

        _lag_epilogue(j, nchunks, pl.num_programs(1), norm_chunk)

    _lag_run(acc0_ref, acc1_ref, step)


def _out_proj_ln1(o_r, o_m, w_out_b, x_raw, eg, eb, g, b):
    n, d = x_raw.shape
    ka = o_r.shape[1]
    tn = 1024
    ns = d // tn
    tm, mt, nchunks, cur, chunk = _lag_specs(n, ns)
    row = lambda a: a.reshape(1, d)
    vec = pl.BlockSpec((1, d), lambda i, j: (0, 0))
    return pl.pallas_call(
        functools.partial(_out_proj_kernel, tn=tn, nchunks=nchunks),
        out_shape=[jax.ShapeDtypeStruct((n, d), F32), jax.ShapeDtypeStruct((n, d), BF16)],
        grid=(mt + 1, ns),
        in_specs=[pl.BlockSpec((tm, ka), lambda i, j: (cur(i), 0), pipeline_mode=pl.Buffered(1)),
                  pl.BlockSpec((tm, ka), lambda i, j: (cur(i), 0), pipeline_mode=pl.Buffered(1)),
                  pl.BlockSpec((ka, tn), lambda i, j: (0, j)),
                  pl.BlockSpec((ka, tn), lambda i, j: (1, j)),
                  pl.BlockSpec((LANES, d), lambda i, j: (chunk(i, j), 0)),
                  vec, vec, vec, vec],
        out_specs=[pl.BlockSpec((LANES, d), lambda i, j: (chunk(i, j), 0)),
                   pl.BlockSpec((LANES, d), lambda i, j: (chunk(i, j), 0))],
        scratch_shapes=[pltpu.VMEM((tm, d), F32), pltpu.VMEM((tm, d), F32)],
        compiler_params=_cparams("arbitrary", "arbitrary"),
        name="out_proj_ln1",
    )(o_r, o_m, w_out_b, w_out_b, x_raw, row(eg), row(eb), row(g), row(b))


def _peer_scores_kernel(x_ref, wq_ref, keys_ref, o_ref):
    q = _dot(x_ref[...], wq_ref[...])
    nk, kd = keys_ref.shape[1], keys_ref.shape[2]
    for hp in range(keys_ref.shape[0]):
        qh = q[:, hp * kd:(hp + 1) * kd].astype(BF16)
        s = _dot_nt(keys_ref[hp], qh)
        for g in range(o_ref.shape[0]):
            o_ref[g, hp] = s[:, g * LANES:(g + 1) * LANES]


def _peer_scores(x1b, wq_b, keys_b):
    n, d = x1b.shape
    hp, nk, kd = keys_b.shape
    tb = _pick(n, 512, 256, 128)
    return pl.pallas_call(
        _peer_scores_kernel,
        out_shape=jax.ShapeDtypeStruct((n // LANES, hp, nk, LANES), F32),
        grid=(n // tb,),
        in_specs=[pl.BlockSpec((tb, d), lambda i: (i, 0)),
                  pl.BlockSpec(wq_b.shape, lambda i: (0, 0), pipeline_mode=pl.Buffered(1)),
                  pl.BlockSpec(keys_b.shape, lambda i: (0, 0, 0))],
        out_specs=pl.BlockSpec((tb // LANES, hp, nk, LANES), lambda i: (i, 0, 0, 0)),
        compiler_params=_cparams("parallel"),
        name="peer_scores",
    )(x1b, wq_b, keys_b)


def _top_ranks(s, k):
    n, tb = s.shape
    idx = lax.broadcasted_iota(I32, (n, tb), 0)
    rank = jnp.full((n, tb), k, I32)
    vals = []
    for r in range(k):
        m = jnp.max(s, axis=0, keepdims=True)
        first = jnp.min(jnp.where(s == m, idx, n), axis=0, keepdims=True)
        hit = idx == first
        rank = jnp.where(hit, r, rank)
        s = jnp.where(hit, -jnp.inf, s)
        vals.append(m)
    return jnp.concatenate(vals, axis=0), rank


def _route_head_exact(s1, s2, topk):
    a, r1 = _top_ranks(s1, topk)
    b, r2 = _top_ranks(s2, topk)
    cand = jnp.concatenate([a[r:r + 1, :] + b for r in range(topk)], axis=0)
    tv, rc = _top_ranks(cand, topk)
    z = jnp.sum(jnp.exp(tv - tv[0:1, :]), axis=0, keepdims=True)
    c1 = jnp.zeros(s1.shape, I32)
    for r in range(topk):
        sel = rc[r * topk:(r + 1) * topk, :] < topk
        c1 = jnp.where(r1 == r, jnp.sum(sel.astype(I32), axis=0, keepdims=True), c1)
    return jnp.exp(s1 - a[0:1, :]) / z, c1, jnp.exp(s2 - b[0:1, :]), r2


def _top_values(s, k, on_hit):
    vals = []
    for r in range(k):
        m = jnp.max(s, axis=0, keepdims=True)
        hit = s == m
        on_hit(r, hit)
        s = jnp.where(hit, -jnp.inf, s)
        vals.append(m)
    return vals


def _route_head_fast(s1, s2, topk):
    n, tb = s1.shape
    sub = 8
    assert topk == 2 * sub
    st = dict(r1=jnp.full((n, tb), topk, I32), r2=jnp.full((n, tb), topk, I32))

    def hit1(r, hit):
        st["r1"] = jnp.where(hit, r, st["r1"])

    def hit2(r, hit):
        st["r2"] = jnp.where(hit, r, st["r2"])

    a = _top_values(s1, topk, hit1)
    b = _top_values(s2, topk, hit2)
    r1, r2 = st["r1"], st["r2"]
    a_arr = jnp.concatenate(a, axis=0)
    b_arr = jnp.concatenate(b, axis=0)
    row = lax.broadcasted_iota(I32, (sub, tb), 0)
    slabs, cols = [], []
    for q2 in range(sub):
        lim = topk // (q2 + 1)
        for r1s in range(0, lim, sub):
            slab = a_arr[r1s:r1s + sub, :] + b_arr[q2:q2 + 1, :]
            if lim - r1s < sub:
                slab = jnp.where(row < lim - r1s, slab, -jnp.inf)
            slabs.append(slab)
            cols.append((r1s, q2))
    slabs.append(a_arr[0:1, :] + b_arr[sub:topk, :])
    cand = jnp.concatenate(slabs, axis=0)
    st["sel"] = jnp.zeros(cand.shape, I32)

    def hit3(r, hit):
        st["sel"] = jnp.where(hit, 1, st["sel"])

    tv = _top_values(cand, topk, hit3)
    sel = st["sel"]
    count = lambda m: jnp.sum(m.astype(I32), axis=0, keepdims=True)
    rowcount = [jnp.zeros((sub, tb), I32) for _ in range(topk // sub)]
    for i, (r1s, _) in enumerate(cols):
        rowcount[r1s // sub] = rowcount[r1s // sub] + sel[i * sub:(i + 1) * sub, :]
    rowcount[0] = rowcount[0] + jnp.where(row == 0, count(sel[len(cols) * sub:, :]), 0)
    ok = (count(r1 < topk) == topk) & (count(r2 < topk) == topk) & (count(sel) == topk)
    z = jnp.ones_like(tv[0])
    for r in range(1, topk):
        z = z + jnp.exp(tv[r] - tv[0])
    c1 = jnp.zeros((n, tb), I32)
    for r in range(topk):
        c1 = jnp.where(r1 == r, rowcount[r // sub][r % sub:r % sub + 1, :], c1)
    return jnp.exp(s1 - a[0]) / z, c1, jnp.exp(s2 - b[0]), r2, ok


def _peer_route_kernel(s_ref, e1_ref, c1_ref, e2_ref, r2_ref, *, topk):
    heads = e1_ref.shape[0]
    tb = s_ref.shape[2]
    bad = jnp.zeros((1, tb), I32)
    for h in range(heads):
        e1, c1, e2, r2, ok = _route_head_fast(s_ref[2 * h], s_ref[2 * h + 1], topk)
        e1_ref[h], c1_ref[h], e2_ref[h], r2_ref[h] = e1, c1, e2, r2
        bad = jnp.where(ok, bad, 1)

    @pl.when(jnp.max(bad) > 0)
    def _():
        def redo(h, carry):
            e1, c1, e2, r2 = _route_head_exact(s_ref[2 * h], s_ref[2 * h + 1], topk)
            e1_ref[h], c1_ref[h], e2_ref[h], r2_ref[h] = e1, c1, e2, r2
            return carry

        lax.fori_loop(0, heads, redo, 0)


def _peer_route(scores, heads):
    ng, hp, nk, tb = scores.shape
    spec = pl.BlockSpec((None, heads, nk, tb), lambda i: (i, 0, 0, 0))
    shape = (ng, heads, nk, tb)
    return pl.pallas_call(
        functools.partial(_peer_route_kernel, topk=PEER_TOPK),
        out_shape=[jax.ShapeDtypeStruct(shape, F32), jax.ShapeDtypeStruct(shape, I32),
                   jax.ShapeDtypeStruct(shape, F32), jax.ShapeDtypeStruct(shape, I32)],
        grid=(ng,),
        in_specs=[pl.BlockSpec((None, hp, nk, tb), lambda i: (i, 0, 0, 0))],
        out_specs=[spec, spec, spec, spec],
        compiler_params=_cparams("parallel"),
        name="peer_route",
    )(scores)


def _gelu_tanh(x):
    c = math.sqrt(2.0 / math.pi)
    return x * (0.5 * (1.0 + jnp.tanh(c * (x + 0.044715 * (x * x * x)))))


def _peer_dense_kernel(x_ref, u_ref, vt_ref, e1_ref, c1_ref, e2_ref, r2_ref, y_ref, xt_ref, acc_ref, a_ref, *, ti,
                       sub):
    j = pl.program_id(1)
    nj = pl.num_programs(1) - 1
    ngrp, heads, nk, _ = e2_ref.shape
    tb = ngrp * LANES

    @pl.when(j == 0)
    def _():
        acc_ref[...] = jnp.zeros_like(acc_ref)
        a_ref[1] = jnp.zeros(a_ref.shape[1:], a_ref.dtype)
        for c in range(tb // LANES):
            cols = slice(c * LANES, (c + 1) * LANES)
            xt_ref[:, cols] = x_ref[cols, :].astype(F32).T.astype(BF16)

    jc = jnp.minimum(j, nj - 1)
    tiles = [slice(s * sub * nk, (s + 1) * sub * nk) for s in range(ti // sub)]
    acts = [_dot(u_ref[rows, :], xt_ref[...]) for rows in tiles]
    acc_ref[...] += _dot(vt_ref[...], a_ref[(j + 1) % 2])
    for s, rows in enumerate(tiles):
        act = _gelu_tanh(acts[s])
        parts = []
        for t in range(sub):
            i1 = pl.ds(jc * ti + s * sub + t, 1)
            gates = []
            for lg in range(ngrp):
                g = jnp.zeros((nk, LANES), F32)
                for h in range(heads):
                    hit = r2_ref[lg, h] < c1_ref[lg, h, i1, :]
                    g = g + jnp.where(hit, e1_ref[lg, h, i1, :] * e2_ref[lg, h], 0.0)
                gates.append(g)
            parts.append((act[t * nk:(t + 1) * nk, :] * jnp.concatenate(gates, axis=1)).astype(BF16))
        a_ref[j % 2, rows, :] = jnp.concatenate(parts, axis=0)

    @pl.when(j == nj)
    def _():
        y_ref[...] = acc_ref[...].T


def _peer_dense(x1b, u_b, vt_b, e1, c1, e2, r2):
    n, d = x1b.shape
    ne = u_b.shape[0]
    _, heads, nk, _ = e1.shape
    tb = _pick(n, 512, 256, 128)
    te = PEER_TE
    nj = ne // te
    ti = te // nk
    sub = 2
    once = pl.Buffered(1)
    rspec = pl.BlockSpec((tb // LANES, heads, nk, LANES), lambda i, j: (i, 0, 0, 0), pipeline_mode=once)
    return pl.pallas_call(
        functools.partial(_peer_dense_kernel, ti=ti, sub=sub),
        out_shape=jax.ShapeDtypeStruct((n, d), F32),
        grid=(n // tb, nj + 1),
        in_specs=[pl.BlockSpec((tb, d), lambda i, j: (i, 0), pipeline_mode=once),
                  pl.BlockSpec((te, d), lambda i, j: (jnp.minimum(j, nj - 1), 0)),
                  pl.BlockSpec((None, d, te), lambda i, j: (jnp.maximum(j - 1, 0), 0, 0)),
                  rspec, rspec, rspec, rspec],
        out_specs=pl.BlockSpec((tb, d), lambda i, j: (i, 0), pipeline_mode=once),
        scratch_shapes=[pltpu.VMEM((d, tb), BF16), pltpu.VMEM((d, tb), F32), pltpu.VMEM((2, te, tb), BF16)],
        compiler_params=_cparams("parallel", "arbitrary"),
        name="peer_dense",
    )(x1b, u_b, vt_b, e1, c1, e2, r2)


def _cast_kernel(x_ref, o_ref):
    o_ref[...] = x_ref[...].astype(BF16)


def _cast_t_kernel(x_ref, o_ref):
    o_ref[...] = x_ref[...].T.astype(BF16)


def _cast_rows(x, transpose):
    r, d = x.shape
    tr = PEER_TE
    if transpose:
        kern, oshape, ospec = _cast_t_kernel, (r // tr, d, tr), pl.BlockSpec((None, d, tr), lambda i: (i, 0, 0))
    else:
        kern, oshape, ospec = _cast_kernel, (r, d), pl.BlockSpec((tr, d), lambda i: (i, 0))
    return pl.pallas_call(
        kern,
        out_shape=jax.ShapeDtypeStruct(oshape, BF16),
        grid=(r // tr,),
        in_specs=[pl.BlockSpec((tr, d), lambda i: (i, 0))],
        out_specs=ospec,
        compiler_params=_cparams("parallel"),
        name="cast_t" if transpose else "cast",
    )(x)


def _final_kernel(xf_ref, xb_ref, ch_ref, p_ref, wg_ref, wp_ref, g_ref, b_ref, o_ref, acc0_ref, acc1_ref, *, tn,
                  nchunks):
    j = pl.program_id(1)

    def step(fill_ref, done_ref):
        cols = pl.ds(pl.multiple_of(j * tn, tn), tn)
        gate = _sigmoid(_dot(xb_ref[...], wg_ref[...]))
        proj = _dot(p_ref[...].astype(BF16), wp_ref[...])
        fill_ref[:, cols] = DEEPNORM_ALPHA * xf_ref[...] + ch_ref[...] + gate * proj

        def norm_chunk():
            rows = pl.ds(pl.multiple_of(j * LANES, LANES), LANES)
            o_ref[...] = _layer_norm(done_ref[rows, :], g_ref[...], b_ref[...])

        _lag_epilogue(j, nchunks, pl.num_programs(1), norm_chunk)

    _lag_run(acc0_ref, acc1_ref, step)


def _final(x1f, x1b, ch, p, wg_b, wp_b, g, b):
    n, d = x1f.shape
    pd = p.shape[1]
    tn = 1024
    ns = d // tn
    tm, mt, nchunks, cur, chunk = _lag_specs(n, ns)
    vec = pl.BlockSpec((1, d), lambda i, j: (0, 0))
    rows = lambda w: pl.BlockSpec((tm, w), lambda i, j: (cur(i), 0), pipeline_mode=pl.Buffered(1))
    tile = pl.BlockSpec((tm, tn), lambda i, j: (cur(i), j))
    return pl.pallas_call(
        functools.partial(_final_kernel, tn=tn, nchunks=nchunks),
        out_shape=jax.ShapeDtypeStruct((n, d), F32),
        grid=(mt + 1, ns),
        in_specs=[tile, rows(d), tile, rows(pd),
                  pl.BlockSpec((d, tn), lambda i, j: (0, j)),
                  pl.BlockSpec((pd, tn), lambda i, j: (0, j)),
                  vec, vec],
        out_specs=pl.BlockSpec((LANES, d), lambda i, j: (chunk(i, j), 0)),
        scratch_shapes=[pltpu.VMEM((tm, d), F32), pltpu.VMEM((tm, d), F32)],
        compiler_params=_cparams("arbitrary", "arbitrary"),
        name="ple_ln2",
    )(x1f, x1b, ch, p, wg_b, wp_b, g.reshape(1, d), b.reshape(1, d))


def _rope_tables(pos, half):
    inv = ROPE_BASE ** (-jnp.arange(half, dtype=F32) / half)
    ang = pos.astype(F32)[:, None] * inv[None]
    return jnp.cos(ang), jnp.sin(ang)


def _post_mixer(o_r, o_m, x_raw, p, wts):
    x1f, x1b = _out_proj_ln1(o_r, o_m, wts["w_out"], x_raw, wts["ln_emb_g"], wts["ln_emb_b"], wts["ln1_g"],
                             wts["ln1_b"])
    heads = wts["keys"].shape[0] // 2
    scores = _peer_scores(x1b, wts["w_q"], wts["keys"])
    e1, rb, e2, b2 = _peer_route(scores, heads)
    ch = _peer_dense(x1b, wts["u"], wts["vt"], e1, rb, e2, b2)
    return _final(x1f, x1b, ch, p, wts["w_gate"], wts["w_proj"], wts["ln2_g"], wts["ln2_b"])


def kernel(x_prompt, x_sample, state_ret, state_conv, state_mlstm_c, state_mlstm_n, state_mlstm_m, p_prompt, p_sample,
           ln_emb_g, ln_emb_b, w_in, b_gate, conv_w, conv_b, g_ret_norm, g_ml_norm, w_out, ln1_g, ln1_b,
           w_peer_q, peer_sub_keys, peer_u, peer_v, w_ple_gate, w_ple_proj, ln2_g, ln2_b):
    bsz, t, d = x_prompt.shape
    nb = x_sample.shape[0]
    _, _, heads, dk, _ = state_ret.shape
    rw = heads * dk
    n_main = 8 * rw
    assert x_sample.shape[1] == 1 and w_in.shape[0] == DEPTH and t % CHUNK == 0

    w_in_t = jnp.swapaxes(w_in, 1, 2)
    wg_t = w_in_t[0, n_main:, :].astype(BF16)
    log_g = jnp.log1p(-(2.0 ** (-5.0 - jnp.arange(heads, dtype=F32))))
    keys = peer_sub_keys[0]
    wts = dict(
        ln_emb_g=ln_emb_g, ln_emb_b=ln_emb_b, ln1_g=ln1_g[0], ln1_b=ln1_b[0], ln2_g=ln2_g[0], ln2_b=ln2_b[0],
        w_out=w_out[0].astype(BF16), w_q=w_peer_q[0].astype(BF16),
        keys=keys.reshape(keys.shape[0] * 2, keys.shape[2], keys.shape[3]).astype(BF16),
        u=_cast_rows(peer_u[0], False), vt=_cast_rows(peer_v[0], True),
        w_gate=w_ple_gate[0].astype(BF16), w_proj=w_ple_proj[0].astype(BF16),
    )

    xp = x_prompt.reshape(bsz * t, d)
    xn = _ln_cast(xp, ln_emb_g, ln_emb_b)
    z = _in_proj(xn, w_in_t, n_main)
    gates_t = _gate_rows(xn, wg_t)
    cos, sin = _rope_tables(jnp.arange(t), dk // 2)
    o_r, ret_p = _ret_prompt(z, log_g, cos, sin, g_ret_norm[0], bsz, t, heads, dk, 0)
    o_m, c_p, n_p, m_p = _mlstm_prompt(z, gates_t, b_gate[0], conv_w[0], conv_b[0].reshape(1, 2 * rw), g_ml_norm[0],
                                       bsz, t, heads, dk, 4 * rw)
    conv_p = z.reshape(bsz, t, n_main)[:, t - (CONV_W - 1):, 4 * rw:6 * rw]
    y_prompt = _post_mixer(o_r, o_m, xp, p_prompt[0].reshape(bsz * t, -1), wts).reshape(bsz, t, d)

    xs = x_sample.reshape(nb, d)
    xns = _ln_cast(xs, ln_emb_g, ln_emb_b)
    zs = _in_proj(xns, w_in_t, n_main)
    gates_s = _gate_rows(xns, wg_t).T
    cs, sn = _rope_tables(jnp.full((1,), PAST_LEN), dk // 2)
    o_rs, o_ms, ret_s, conv_s, c_s, n_s, m_s = _mix_sample(
        zs, gates_s, log_g, b_gate[0], jnp.concatenate([cs, sn], axis=0), state_ret[0], state_conv[0],
        state_mlstm_c[0], state_mlstm_n[0], state_mlstm_m[0], conv_w[0], conv_b[0], g_ret_norm[0], g_ml_norm[0], heads, dk)
    y_sample = _post_mixer(o_rs, o_ms, xs, p_sample[0].reshape(nb, -1), wts).reshape(nb, 1, d)

    lead = lambda a: a[None]
    return (y_prompt, y_sample, lead(ret_p), lead(conv_p), lead(c_p), lead(n_p), lead(m_p),
            lead(ret_s), lead(conv_s), lead(c_s), lead(n_s), lead(m_s))
```

```python
import functools
import math

import jax
import jax.numpy as jnp
from jax import lax
from jax.experimental import pallas as pl
from jax.experimental.pallas import tpu as pltpu

F32 = jnp.float32
BF16 = jnp.bfloat16
I32 = jnp.int32

LN_EPS = 1e-5
DEPTH = 1
DEEPNORM_ALPHA = (2.0 * DEPTH) ** 0.25
CHUNK = 128
ROPE_BASE = 10000.0
PAST_LEN = 16384
PEER_TOPK = 16
PEER_TE = 512
CONV_W = 4

V7X_VMEM_LIMIT_BYTES = 56 * 1024 * 1024
LANES = 128


def _cparams(*sem):
    return pltpu.CompilerParams(dimension_semantics=sem, vmem_limit_bytes=V7X_VMEM_LIMIT_BYTES)


def _dot(a, b):
    return jnp.dot(a, b, preferred_element_type=F32)


def _dot_nt(a, b):
    return lax.dot_general(a, b, (((1,), (1,)), ((), ())), preferred_element_type=F32)


def _dot_tn(a, b):
    return lax.dot_general(a, b, (((0,), (0,)), ((), ())), preferred_element_type=F32)


def _sigmoid(x):
    return 1.0 / (1.0 + jnp.exp(-x))


def _layer_norm(x, g, b):
    mu = jnp.mean(x, axis=-1, keepdims=True)
    xc = x - mu
    var = jnp.mean(xc * xc, axis=-1, keepdims=True)
    return xc * lax.rsqrt(var + LN_EPS) * g + b


def _head_norm(x, g):
    mu = jnp.mean(x, axis=-1, keepdims=True)
    xc = x - mu
    var = jnp.mean(xc * xc, axis=-1, keepdims=True)
    return xc * lax.rsqrt(var + LN_EPS) * g


def _pick(n, *cands):
    for c in cands:
        if n % c == 0:
            return c
    return n


def _ln_cast_kernel(x_ref, g_ref, b_ref, o_ref):
    o_ref[...] = _layer_norm(x_ref[...], g_ref[...], b_ref[...]).astype(BF16)


def _ln_cast(x, g, b):
    n, d = x.shape
    tm = _pick(n, 256, 128)
    return pl.pallas_call(
        _ln_cast_kernel,
        out_shape=jax.ShapeDtypeStruct((n, d), BF16),
        grid=(n // tm,),
        in_specs=[pl.BlockSpec((tm, d), lambda i: (i, 0)),
                  pl.BlockSpec((1, d), lambda i: (0, 0)),
                  pl.BlockSpec((1, d), lambda i: (0, 0))],
        out_specs=pl.BlockSpec((tm, d), lambda i: (i, 0)),
        compiler_params=_cparams("parallel"),
        name="ln_cast",
    )(x, g.reshape(1, d), b.reshape(1, d))


def _in_proj_kernel(x_ref, w_ref, o_ref, wb_ref):
    @pl.when(pl.program_id(1) == 0)
    def _():
        wb_ref[...] = w_ref[...].astype(BF16)

    o_ref[...] = _dot_nt(x_ref[...], wb_ref[...])


def _in_proj(xn, w_in_t, n_main):
    n, d = xn.shape
    tm = _pick(n, 1024, 512, 256, 128)
    tn = 512
    return pl.pallas_call(
        _in_proj_kernel,
        out_shape=jax.ShapeDtypeStruct((n, n_main), F32),
        grid=(n_main // tn, n // tm),
        in_specs=[pl.BlockSpec((tm, d), lambda j, i: (i, 0)),
                  pl.BlockSpec((None, tn, d), lambda j, i: (0, j, 0))],
        out_specs=pl.BlockSpec((tm, tn), lambda j, i: (i, j)),
        scratch_shapes=[pltpu.VMEM((tn, d), BF16)],
        compiler_params=_cparams("parallel", "arbitrary"),
        name="in_proj",
    )(xn, w_in_t)


def _gate_rows_kernel(w_ref, x_ref, o_ref):
    o_ref[...] = _dot_nt(w_ref[...], x_ref[...])


def _gate_rows(xn, wg_t):
    n, d = xn.shape
    g = wg_t.shape[0]
    tb = _pick(n, 1024, 512, 256, 128)
    return pl.pallas_call(
        _gate_rows_kernel,
        out_shape=jax.ShapeDtypeStruct((g, n), F32),
        grid=(n // tb,),
        in_specs=[pl.BlockSpec((g, d), lambda i: (0, 0)),
                  pl.BlockSpec((tb, d), lambda i: (i, 0))],
        out_specs=pl.BlockSpec((g, tb), lambda i: (0, i)),
        compiler_params=_cparams("parallel"),
        name="gate_rows",
    )(wg_t, xn)


def _rope(x, cos, sin):
    half = x.shape[-1] // 2
    x1, x2 = x[:, :half], x[:, half:]
    return jnp.concatenate([x1 * cos - x2 * sin, x1 * sin + x2 * cos], axis=-1)


def _log_sigmoid(x):
    return jnp.minimum(x, 0.0) - jnp.log1p(jnp.exp(-jnp.abs(x)))


def _row_to_col(row, eye):
    return jnp.sum(jnp.where(eye, row, 0.0), axis=1, keepdims=True)


HEAD_GROUP = 2


def _ret_prompt_kernel(lg_ref, q_ref, k_ref, v_ref, g_ref, cos_ref, sin_ref, gn_ref, o_ref, s_ref, *, chunk, dk):
    grp = pl.program_id(1)
    L = chunk
    t = q_ref.shape[0]
    hg = q_ref.shape[1] // dk
    ii = lax.broadcasted_iota(I32, (L, L), 0)
    jj = lax.broadcasted_iota(I32, (L, L), 1)
    causal = ii >= jj
    diff = jnp.where(causal, (ii - jj).astype(F32), 0.0)
    idx = lax.broadcasted_iota(I32, (L, 1), 0).astype(F32)
    scale = dk ** -0.5
    s_ref[...] = jnp.zeros_like(s_ref)
    per_head = []
    for u in range(hg):
        lg = lg_ref[grp * hg + u]
        per_head.append(dict(
            decay_in=jnp.where(causal, jnp.exp(lg * diff), 0.0),
            decay_q=jnp.exp(lg * (idx + 1.0)),
            decay_k=jnp.exp(lg * (float(L) - 1.0 - idx)),
            decay_c=jnp.exp(lg * jnp.full((1, 1), float(L), F32)),
            gn=gn_ref[pl.ds(grp * hg + u, 1), :]))

    def body(c, carry):
        rows = pl.ds(pl.multiple_of(c * L, L), L)
        cos, sin = cos_ref[rows, :], sin_ref[rows, :]
        for u, hd in enumerate(per_head):
            cols = slice(u * dk, (u + 1) * dk)
            rq = _rope(q_ref[rows, cols], cos, sin)
            rk = _rope(k_ref[rows, cols], cos, sin) * scale
            vb = v_ref[rows, cols].astype(BF16)
            rqb = rq.astype(BF16)
            s = s_ref[u]
            sc = _dot_nt(rqb, rk.astype(BF16)) * hd["decay_in"]
            o = _dot(sc.astype(BF16), vb) + _dot(rqb, s.astype(BF16)) * hd["decay_q"]
            s_ref[u] = s * hd["decay_c"] + _dot_tn((rk * hd["decay_k"]).astype(BF16), vb)
            g = g_ref[rows, cols]
            o_ref[rows, cols] = (_head_norm(o, hd["gn"]) * (g * _sigmoid(g))).astype(o_ref.dtype)
        return carry

    lax.fori_loop(0, t // L, body, 0)


def _ret_prompt(z, log_g, cos, sin, g_norm, bsz, t, heads, dk, col0):
    hg = HEAD_GROUP if heads % HEAD_GROUP == 0 else 1
    ng = heads // hg
    gb = col0 // (hg * dk)
    zspec = lambda off: pl.BlockSpec((t, hg * dk), lambda b, g, *_: (b, gb + off * ng + g))
    grid_spec = pltpu.PrefetchScalarGridSpec(
        num_scalar_prefetch=1,
        grid=(bsz, ng),
        in_specs=[zspec(0), zspec(1), zspec(2), zspec(3),
                  pl.BlockSpec((t, dk // 2), lambda b, g, *_: (0, 0)),
                  pl.BlockSpec((t, dk // 2), lambda b, g, *_: (0, 0)),
                  pl.BlockSpec((heads, dk), lambda b, g, *_: (0, 0))],
        out_specs=[pl.BlockSpec((t, hg * dk), lambda b, g, *_: (b, g)),
                   pl.BlockSpec((None, hg, dk, dk), lambda b, g, *_: (b, g, 0, 0))],
    )
    return pl.pallas_call(
        functools.partial(_ret_prompt_kernel, chunk=CHUNK, dk=dk),
        out_shape=[jax.ShapeDtypeStruct((bsz * t, heads * dk), BF16),
                   jax.ShapeDtypeStruct((bsz, heads, dk, dk), F32)],
        grid_spec=grid_spec,
        compiler_params=_cparams("parallel", "parallel"),
        name="ret_prompt",
    )(log_g, z, z, z, z, cos, sin, g_norm)


def _mlstm_prompt_kernel(bg_ref, xq_ref, xk_ref, v_ref, og_ref, ig_ref, fg_ref, cwq_ref, cwk_ref, cbq_ref, cbk_ref,
                         gn_ref, o_ref, c_ref, n_ref, m_ref, bt_s, ig_s, *, chunk, heads, dk):
    grp = pl.program_id(1)
    L = chunk
    t = xq_ref.shape[0]
    hg = xq_ref.shape[1] // dk
    nc = t // L
    scale = dk ** -0.5

    lane = lax.broadcasted_iota(I32, (nc, L), 1)
    for u in range(hg):
        h = grp * hg + u
        ig_s[u] = ig_ref[u] + bg_ref[h]
        bt = _log_sigmoid(fg_ref[u] + bg_ref[heads + h])
        s = 1
        while s < L:
            bt = bt + jnp.where(lane >= s, pltpu.roll(bt, s, axis=1), 0.0)
            s *= 2
        bt_s[u] = bt

    ii = lax.broadcasted_iota(I32, (L, L), 0)
    jj = lax.broadcasted_iota(I32, (L, L), 1)
    causal = ii >= jj
    eye = ii == jj
    row = lax.broadcasted_iota(I32, (L, 1), 0)
    gns = [gn_ref[pl.ds(grp * hg + u, 1), :] for u in range(hg)]
    c_ref[...] = jnp.zeros_like(c_ref)
    n_ref[...] = jnp.zeros_like(n_ref)

    def conv_silu(x_ref, w_ref, b_ref, c, rows, cols):
        x = x_ref[rows, cols]
        prev_rows = pl.ds(pl.multiple_of(jnp.maximum(c - 1, 0) * L, L), L)
        xp = jnp.where(c > 0, x_ref[prev_rows, cols], 0.0)
        y = x * w_ref[CONV_W - 1:CONV_W, cols] + b_ref[:, cols]
        for j in range(1, CONV_W):
            xs = jnp.where(row < j, pltpu.roll(xp, j, axis=0), pltpu.roll(x, j, axis=0))
            y = y + xs * w_ref[CONV_W - 1 - j:CONV_W - j, cols]
        return y * _sigmoid(y)

    def head_step(u, c, rows, m):
        cols = slice(u * dk, (u + 1) * dk)
        q = conv_silu(xq_ref, cwq_ref, cbq_ref, c, rows, cols)
        k = conv_silu(xk_ref, cwk_ref, cbk_ref, c, rows, cols) * scale
        vb = v_ref[rows, cols].astype(BF16)
        qb = q.astype(BF16)
        bt_row = bt_s[u, pl.ds(c, 1), :]
        ig_row = ig_s[u, pl.ds(c, 1), :]
        bt_col = _row_to_col(bt_row, eye)
        ig_col = _row_to_col(ig_row, eye)
        dmat = jnp.where(causal, bt_col - bt_row + ig_row, -jnp.inf)
        prior = bt_col + m
        mt = jnp.maximum(prior, jnp.max(dmat, axis=1, keepdims=True))
        w = jnp.exp(dmat - mt)
        wp = jnp.exp(prior - mt)
        qk = _dot_nt(qb, k.astype(BF16)) * w
        cst = c_ref[u]
        nst = n_ref[u]
        num = _dot(qk.astype(BF16), vb) + _dot(qb, cst.astype(BF16)) * wp
        den = jnp.sum(qk, axis=1, keepdims=True) + jnp.sum(q * nst, axis=1, keepdims=True) * wp
        hh = num / jnp.maximum(jnp.abs(den), jnp.exp(-mt))
        bl = bt_row[:, L - 1:L]
        m_new = mt[L - 1:L, :]
        wk = jnp.exp(bl - bt_col + ig_col - m_new)
        wc = jnp.exp(bl + m - m_new)
        kw = k * wk
        c_ref[u] = cst * wc + _dot_tn(kw.astype(BF16), vb)
        n_ref[u] = nst * wc + jnp.sum(kw, axis=0, keepdims=True)
        og = og_ref[rows, cols]
        o_ref[rows, cols] = (_head_norm(hh, gns[u]) * _sigmoid(og)).astype(o_ref.dtype)
        return m_new

    def body(c, ms):
        rows = pl.ds(pl.multiple_of(c * L, L), L)
        return tuple(head_step(u, c, rows, ms[u]) for u in range(hg))

    m_fin = lax.fori_loop(0, nc, body, tuple(jnp.zeros((1, 1), F32) for _ in range(hg)))
    for u in range(hg):
        m_ref[u] = jnp.broadcast_to(m_fin[u], m_ref.shape[1:])


def _mlstm_prompt(z, gates_t, b_gate, conv_w, conv_b, g_norm, bsz, t, heads, dk, col0):
    hg = HEAD_GROUP if heads % HEAD_GROUP == 0 else 1
    ng = heads // hg
    gb = col0 // (hg * dk)
    nc = t // CHUNK
    zspec = lambda off: pl.BlockSpec((t, hg * dk), lambda b, g, *_: (b, gb + off * ng + g))
    gates4 = gates_t.reshape(2 * heads, bsz, nc, CHUNK)
    gspec = lambda off: pl.BlockSpec((hg, None, nc, CHUNK), lambda b, g, *_: (off * ng + g, b, 0, 0))
    wspec = lambda rows, off: pl.BlockSpec((rows, hg * dk), lambda b, g, *_: (0, off * ng + g))
    grid_spec = pltpu.PrefetchScalarGridSpec(
        num_scalar_prefetch=1,
        grid=(bsz, ng),
        in_specs=[zspec(0), zspec(1), zspec(2), zspec(3), gspec(0), gspec(1),
                  wspec(CONV_W, 0), wspec(CONV_W, 1), wspec(1, 0), wspec(1, 1),
                  pl.BlockSpec((heads, dk), lambda b, g, *_: (0, 0))],
        out_specs=[pl.BlockSpec((t, hg * dk), lambda b, g, *_: (b, g)),
                   pl.BlockSpec((None, hg, dk, dk), lambda b, g, *_: (b, g, 0, 0)),
                   pl.BlockSpec((None, hg, 1, dk), lambda b, g, *_: (b, g, 0, 0)),
                   pl.BlockSpec((None, hg, 1, LANES), lambda b, g, *_: (b, g, 0, 0))],
        scratch_shapes=[pltpu.VMEM((hg, nc, CHUNK), F32), pltpu.VMEM((hg, nc, CHUNK), F32)],
    )
    o, c, n, m = pl.pallas_call(
        functools.partial(_mlstm_prompt_kernel, chunk=CHUNK, heads=heads, dk=dk),
        out_shape=[jax.ShapeDtypeStruct((bsz * t, heads * dk), BF16),
                   jax.ShapeDtypeStruct((bsz, heads, dk, dk), F32),
                   jax.ShapeDtypeStruct((bsz, heads, 1, dk), F32),
                   jax.ShapeDtypeStruct((bsz, heads, 1, LANES), F32)],
        grid_spec=grid_spec,
        compiler_params=_cparams("parallel", "parallel"),
        name="mlstm_prompt",
    )(b_gate, z, z, z, z, gates4, gates4, conv_w, conv_w, conv_b, conv_b, g_norm)
    return o, c, n[:, :, 0, :], m[:, :, 0, 0]


def _pad_rows(row, rows=8):
    r = lax.broadcasted_iota(I32, (rows, row.shape[1]), 0)
    return jnp.where(r == 0, row, 0.0)


def _mix_sample_kernel(lg_ref, bg_ref, z_ref, gt_ref, cs_ref, sr_ref, cb_ref, cc_ref, cn_ref, cm_ref, cw_ref, cbias_ref,
                       gr_ref, gm_ref, or_ref, om_ref, sro_ref, cbo_ref, cco_ref, cno_ref, cmo_ref, *, heads, dk):
    rw = heads * dk
    cos, sin = cs_ref[0:1, :], cs_ref[1:2, :]
    scale = dk ** -0.5
    xqk = z_ref[:, 4 * rw:6 * rw]
    buf = cb_ref[...]
    y = xqk * cw_ref[CONV_W - 1:CONV_W, :] + cbias_ref[...]
    for j in range(CONV_W - 1):
        y = y + buf[j:j + 1, :] * cw_ref[j:j + 1, :]
    qk_act = y * _sigmoid(y)
    cbo_ref[0:CONV_W - 2, :] = buf[1:CONV_W - 1, :]
    cbo_ref[CONV_W - 2:CONV_W - 1, :] = xqk

    for h in range(heads):
        sl = lambda g: slice(g * rw + h * dk, g * rw + (h + 1) * dk)
        gam = jnp.exp(jnp.full((1, 1), lg_ref[h], F32))
        rq = _rope(z_ref[:, sl(0)], cos, sin)
        rk = _rope(z_ref[:, sl(1)], cos, sin) * scale
        v = z_ref[:, sl(2)]
        rg = z_ref[:, sl(3)]
        s = sr_ref[h]
        sc = jnp.sum(rq * rk, axis=1, keepdims=True)
        qs = _dot(jnp.broadcast_to(rq, (8, dk)).astype(BF16), s.astype(BF16))[0:1, :]
        o = sc * v + qs * gam
        sro_ref[h] = s * gam + _dot_tn(_pad_rows(rk).astype(BF16), jnp.broadcast_to(v, (8, dk)).astype(BF16))
        or_ref[:, h * dk:(h + 1) * dk] = (_head_norm(o, gr_ref[h:h + 1, :]) * (rg * _sigmoid(rg))).astype(or_ref.dtype)
        q = qk_act[:, h * dk:(h + 1) * dk]
        k = qk_act[:, rw + h * dk:rw + (h + 1) * dk] * scale
        v = z_ref[:, sl(6)]
        og = z_ref[:, sl(7)]
        it = gt_ref[:, h:h + 1] + bg_ref[h]
        lf = _log_sigmoid(gt_ref[:, heads + h:heads + h + 1] + bg_ref[heads + h])
        m = cm_ref[:, h:h + 1]
        cst = cc_ref[h]
        nst = cn_ref[h:h + 1, :]
        prior = lf + m
        mt = jnp.maximum(prior, it)
        w = jnp.exp(it - mt)
        wp = jnp.exp(prior - mt)
        qk = jnp.sum(q * k, axis=1, keepdims=True) * w
        qc = _dot(jnp.broadcast_to(q, (8, dk)).astype(BF16), cst.astype(BF16))[0:1, :]
        num = qk * v + qc * wp
        den = qk + jnp.sum(q * nst, axis=1, keepdims=True) * wp
        hh = num / jnp.maximum(jnp.abs(den), jnp.exp(-mt))
        wk = jnp.exp(it - mt)
        wc = jnp.exp(lf + m - mt)
        kw = k * wk
        cco_ref[h] = cst * wc + _dot_tn(_pad_rows(kw).astype(BF16), jnp.broadcast_to(v, (8, dk)).astype(BF16))
        cno_ref[h:h + 1, :] = nst * wc + kw
        cmo_ref[:, h:h + 1] = mt
        om_ref[:, h * dk:(h + 1) * dk] = (_head_norm(hh, gm_ref[h:h + 1, :]) * _sigmoid(og)).astype(om_ref.dtype)


def _mix_sample(z, gates, log_g, b_gate, cos_sin, s_ret, s_conv, s_c, s_n, s_m, conv_w, conv_b, g_ret, g_ml, heads, dk):
    nb = z.shape[0]
    rw = heads * dk
    per_b3 = lambda *tail: pl.BlockSpec((None,) + tail, lambda b, *_: (b,) + (0,) * len(tail))
    whole = lambda a: pl.BlockSpec(a.shape, lambda b, *_: (0,) * a.ndim)
    z3 = z.reshape(nb, 1, 8 * rw)
    g3 = gates.reshape(nb, 1, 2 * heads)
    m3 = s_m.reshape(nb, 1, heads)
    cb2 = conv_b.reshape(1, 2 * rw)
    grid_spec = pltpu.PrefetchScalarGridSpec(
        num_scalar_prefetch=2,
        grid=(nb,),
        in_specs=[per_b3(1, 8 * rw), per_b3(1, 2 * heads), whole(cos_sin),
                  per_b3(heads, dk, dk), per_b3(CONV_W - 1, 2 * rw), per_b3(heads, dk, dk), per_b3(heads, dk),
                  per_b3(1, heads), whole(conv_w), whole(cb2), whole(g_ret), whole(g_ml)],
        out_specs=[per_b3(1, rw), per_b3(1, rw), per_b3(heads, dk, dk), per_b3(CONV_W - 1, 2 * rw),
                   per_b3(heads, dk, dk), per_b3(heads, dk), per_b3(1, heads)],
    )
    o_r, o_m, sr, cb, cc, cn, cm = pl.pallas_call(
        functools.partial(_mix_sample_kernel, heads=heads, dk=dk),
        out_shape=[jax.ShapeDtypeStruct((nb, 1, rw), BF16), jax.ShapeDtypeStruct((nb, 1, rw), BF16),
                   jax.ShapeDtypeStruct(s_ret.shape, F32), jax.ShapeDtypeStruct(s_conv.shape, F32),
                   jax.ShapeDtypeStruct(s_c.shape, F32), jax.ShapeDtypeStruct(s_n.shape, F32),
                   jax.ShapeDtypeStruct((nb, 1, heads), F32)],
        grid_spec=grid_spec,
        compiler_params=_cparams("parallel"),
        name="mix_sample",
    )(log_g, b_gate, z3, g3, cos_sin, s_ret, s_conv, s_c, s_n, m3, conv_w, cb2, g_ret, g_ml)
    return o_r.reshape(nb, rw), o_m.reshape(nb, rw), sr, cb, cc, cn, cm.reshape(nb, heads)


def _lag_specs(n, ns):
    tm = _pick(n, *(c for c in (512, 256, 128) if c // LANES <= ns))
    mt = n // tm
    nchunks = tm // LANES
    cur = lambda i: jnp.minimum(i, mt - 1)
    chunk = lambda i, j: jnp.maximum(i - 1, 0) * nchunks + jnp.where(i == 0, 0, jnp.minimum(j, nchunks - 1))
    return tm, mt, nchunks, cur, chunk


def _lag_epilogue(j, nchunks, ns, fn):
    if nchunks == ns:
        fn()
    else:
        pl.when(j < nchunks)(fn)


def _lag_run(acc0_ref, acc1_ref, step):
    i, j = pl.program_id(0), pl.program_id(1)

    @pl.when((i == 0) & (j == 0))
    def _():
        acc1_ref[...] = jnp.zeros_like(acc1_ref)

    @pl.when(i % 2 == 0)
    def _():
        step(acc0_ref, acc1_ref)

    @pl.when(i % 2 == 1)
    def _():
        step(acc1_ref, acc0_ref)


def _out_proj_kernel(a_ref, b_ref, wa_ref, wb_ref, x_ref, eg_ref, eb_ref, g_ref, bb_ref, of_ref, ob_ref, ot_ref,
                     acc0_ref, acc1_ref, *, tn, nchunks):
    j = pl.program_id(1)

    def step(fill_ref, done_ref):
        cols = pl.ds(pl.multiple_of(j * tn, tn), tn)
        fill_ref[:, cols] = _dot(a_ref[...], wa_ref[...]) + _dot(b_ref[...], wb_ref[...])

        def norm_chunk():
            rows = pl.ds(pl.multiple_of(j * LANES, LANES), LANES)
            xe = _layer_norm(x_ref[...], eg_ref[...], eb_ref[...])
            x1 = _layer_norm(DEEPNORM_ALPHA * xe + done_ref[rows, :], g_ref[...], bb_ref[...])
            of_ref[...] = x1
            ob_ref[...] = x1.astype(BF16)
            ot_ref[...] = x1.T.astype(BF16)

        _lag_epilogue(j, nchunks, pl.num_programs(1), norm_chunk)

    _lag_run(acc0_ref, acc1_ref, step)


def _out_proj_ln1(o_r, o_m, w_out_b, x_raw, eg, eb, g, b):
    n, d = x_raw.shape
    ka = o_r.shape[1]
    tn = 1024
    ns = d // tn
    tm, mt, nchunks, cur, chunk = _lag_specs(n, ns)
    row = lambda a: a.reshape(1, d)
    vec = pl.BlockSpec((1, d), lambda i, j: (0, 0))
    return pl.pallas_call(
        functools.partial(_out_proj_kernel, tn=tn, nchunks=nchunks),
        out_shape=[jax.ShapeDtypeStruct((n, d), F32), jax.ShapeDtypeStruct((n, d), BF16),
                   jax.ShapeDtypeStruct((d, n), BF16)],
        grid=(mt + 1, ns),
        in_specs=[pl.BlockSpec((tm, ka), lambda i, j: (cur(i), 0), pipeline_mode=pl.Buffered(1)),
                  pl.BlockSpec((tm, ka), lambda i, j: (cur(i), 0), pipeline_mode=pl.Buffered(1)),
                  pl.BlockSpec((ka, tn), lambda i, j: (0, j)),
                  pl.BlockSpec((ka, tn), lambda i, j: (1, j)),
                  pl.BlockSpec((LANES, d), lambda i, j: (chunk(i, j), 0)),
                  vec, vec, vec, vec],
        out_specs=[pl.BlockSpec((LANES, d), lambda i, j: (chunk(i, j), 0)),
                   pl.BlockSpec((LANES, d), lambda i, j: (chunk(i, j), 0)),
                   pl.BlockSpec((d, LANES), lambda i, j: (0, chunk(i, j)))],
        scratch_shapes=[pltpu.VMEM((tm, d), F32), pltpu.VMEM((tm, d), F32)],
        compiler_params=_cparams("arbitrary", "arbitrary"),
        name="out_proj_ln1",
    )(o_r, o_m, w_out_b, w_out_b, x_raw, row(eg), row(eb), row(g), row(b))


def _peer_scores_kernel(x_ref, wq_ref, keys_ref, o_ref):
    q = _dot(x_ref[...], wq_ref[...])
    nk, kd = keys_ref.shape[1], keys_ref.shape[2]
    for hp in range(keys_ref.shape[0]):
        qh = q[:, hp * kd:(hp + 1) * kd].astype(BF16)
        o_ref[hp] = _dot_nt(keys_ref[hp], qh)


def _peer_scores(x1b, wq_b, keys_b):
    n, d = x1b.shape
    hp, nk, kd = keys_b.shape
    tb = _pick(n, 512, 256, 128)
    return pl.pallas_call(
        _peer_scores_kernel,
        out_shape=jax.ShapeDtypeStruct((hp, nk, n), F32),
        grid=(n // tb,),
        in_specs=[pl.BlockSpec((tb, d), lambda i: (i, 0)),
                  pl.BlockSpec(wq_b.shape, lambda i: (0, 0), pipeline_mode=pl.Buffered(1)),
                  pl.BlockSpec(keys_b.shape, lambda i: (0, 0, 0))],
        out_specs=pl.BlockSpec((hp, nk, tb), lambda i: (0, 0, i)),
        compiler_params=_cparams("parallel"),
        name="peer_scores",
    )(x1b, wq_b, keys_b)


def _top_ranks(s, k):
    n, tb = s.shape
    idx = lax.broadcasted_iota(I32, (n, tb), 0)
    rank = jnp.full((n, tb), k, I32)
    vals = []
    for r in range(k):
        m = jnp.max(s, axis=0, keepdims=True)
        first = jnp.min(jnp.where(s == m, idx, n), axis=0, keepdims=True)
        hit = idx == first
        rank = jnp.where(hit, r, rank)
        s = jnp.where(hit, -jnp.inf, s)
        vals.append(m)
    return jnp.concatenate(vals, axis=0), rank


def _route_head_exact(s1, s2, topk):
    a, r1 = _top_ranks(s1, topk)
    b, r2 = _top_ranks(s2, topk)
    cand = jnp.concatenate([a[r:r + 1, :] + b for r in range(topk)], axis=0)
    tv, rc = _top_ranks(cand, topk)
    z = jnp.sum(jnp.exp(tv - tv[0:1, :]), axis=0, keepdims=True)
    c1 = jnp.zeros(s1.shape, I32)
    for r in range(topk):
        sel = rc[r * topk:(r + 1) * topk, :] < topk
        c1 = jnp.where(r1 == r, jnp.sum(sel.astype(I32), axis=0, keepdims=True), c1)
    return jnp.exp(s1 - a[0:1, :]) / z, c1, jnp.exp(s2 - b[0:1, :]), r2


def _top_values(s, k, on_hit):
    vals = []
    for r in range(k):
        m = jnp.max(s, axis=0, keepdims=True)
        hit = s == m
        on_hit(r, hit)
        s = jnp.where(hit, -jnp.inf, s)
        vals.append(m)
    return vals


def _route_head_fast(s1, s2, topk):
    n, tb = s1.shape
    sub = 8
    assert topk == 2 * sub
    st = dict(r1=jnp.full((n, tb), topk, I32), r2=jnp.full((n, tb), topk, I32))

    def hit1(r, hit):
        st["r1"] = jnp.where(hit, r, st["r1"])

    def hit2(r, hit):
        st["r2"] = jnp.where(hit, r, st["r2"])

    a = _top_values(s1, topk, hit1)
    b = _top_values(s2, topk, hit2)
    r1, r2 = st["r1"], st["r2"]
    a_arr = jnp.concatenate(a, axis=0)
    b_arr = jnp.concatenate(b, axis=0)
    row = lax.broadcasted_iota(I32, (sub, tb), 0)
    slabs, cols = [], []
    for q2 in range(sub):
        lim = topk // (q2 + 1)
        for r1s in range(0, lim, sub):
            slab = a_arr[r1s:r1s + sub, :] + b_arr[q2:q2 + 1, :]
            if lim - r1s < sub:
                slab = jnp.where(row < lim - r1s, slab, -jnp.inf)
            slabs.append(slab)
            cols.append((r1s, q2))
    slabs.append(a_arr[0:1, :] + b_arr[sub:topk, :])
    cand = jnp.concatenate(slabs, axis=0)
    st["sel"] = jnp.zeros(cand.shape, I32)

    def hit3(r, hit):
        st["sel"] = jnp.where(hit, 1, st["sel"])

    tv = _top_values(cand, topk, hit3)
    sel = st["sel"]
    count = lambda m: jnp.sum(m.astype(I32), axis=0, keepdims=True)
    rowcount = [jnp.zeros((sub, tb), I32) for _ in range(topk // sub)]
    for i, (r1s, _) in enumerate(cols):
        rowcount[r1s // sub] = rowcount[r1s // sub] + sel[i * sub:(i + 1) * sub, :]
    rowcount[0] = rowcount[0] + jnp.where(row == 0, count(sel[len(cols) * sub:, :]), 0)
    ok = (count(r1 < topk) == topk) & (count(r2 < topk) == topk) & (count(sel) == topk)
    z = jnp.ones_like(tv[0])
    for r in range(1, topk):
        z = z + jnp.exp(tv[r] - tv[0])
    c1 = jnp.zeros((n, tb), I32)
    for r in range(topk):
        c1 = jnp.where(r1 == r, rowcount[r // sub][r % sub:r % sub + 1, :], c1)
    return jnp.exp(s1 - a[0]) / z, c1, jnp.exp(s2 - b[0]), r2, ok


def _peer_route_kernel(s_ref, e1_ref, c1_ref, e2_ref, r2_ref, *, topk):
    heads = e1_ref.shape[0]
    tb = s_ref.shape[2]
    bad = jnp.zeros((1, tb), I32)
    for h in range(heads):
        e1, c1, e2, r2, ok = _route_head_fast(s_ref[2 * h], s_ref[2 * h + 1], topk)
        e1_ref[h], c1_ref[h], e2_ref[h], r2_ref[h] = e1, c1, e2, r2
        bad = jnp.where(ok, bad, 1)

    @pl.when(jnp.max(bad) > 0)
    def _():
        def redo(h, carry):
            e1, c1, e2, r2 = _route_head_exact(s_ref[2 * h], s_ref[2 * h + 1], topk)
            e1_ref[h], c1_ref[h], e2_ref[h], r2_ref[h] = e1, c1, e2, r2
            return carry

        lax.fori_loop(0, heads, redo, 0)


def _peer_route(scores, heads):
    hp, nk, n = scores.shape
    tb = LANES
    spec = pl.BlockSpec((heads, nk, tb), lambda i: (0, 0, i))
    return pl.pallas_call(
        functools.partial(_peer_route_kernel, topk=PEER_TOPK),
        out_shape=[jax.ShapeDtypeStruct((heads, nk, n), F32), jax.ShapeDtypeStruct((heads, nk, n), I32),
                   jax.ShapeDtypeStruct((heads, nk, n), F32), jax.ShapeDtypeStruct((heads, nk, n), I32)],
        grid=(n // tb,),
        in_specs=[pl.BlockSpec((hp, nk, tb), lambda i: (0, 0, i))],
        out_specs=[spec, spec, spec, spec],
        compiler_params=_cparams("parallel"),
        name="peer_route",
    )(scores)


def _gelu_tanh(x):
    c = math.sqrt(2.0 / math.pi)
    return x * (0.5 * (1.0 + jnp.tanh(c * (x + 0.044715 * (x * x * x)))))


def _peer_dense_kernel(xt_ref, u_ref, vt_ref, e1_ref, c1_ref, e2_ref, r2_ref, y_ref, acc_ref, a_ref, *, ti, sub):
    j = pl.program_id(1)
    nj = pl.num_programs(1) - 1
    heads, nk, tb = e2_ref.shape

    @pl.when(j == 0)
    def _():
        acc_ref[...] = jnp.zeros_like(acc_ref)
        a_ref[1] = jnp.zeros(a_ref.shape[1:], a_ref.dtype)

    jc = jnp.minimum(j, nj - 1)
    tiles = [slice(s * sub * nk, (s + 1) * sub * nk) for s in range(ti // sub)]
    acts = [_dot(u_ref[rows, :], xt_ref[...]) for rows in tiles]
    acc_ref[...] += _dot(vt_ref[...], a_ref[(j + 1) % 2])
    for s, rows in enumerate(tiles):
        act = _gelu_tanh(acts[s])
        parts = []
        for t in range(sub):
            i1 = pl.ds(jc * ti + s * sub + t, 1)
            g = jnp.zeros((nk, tb), F32)
            for h in range(heads):
                hit = r2_ref[h] < c1_ref[h, i1, :]
                g = g + jnp.where(hit, e1_ref[h, i1, :] * e2_ref[h], 0.0)
            parts.append((act[t * nk:(t + 1) * nk, :] * g).astype(BF16))
        a_ref[j % 2, rows, :] = jnp.concatenate(parts, axis=0)

    @pl.when(j == nj)
    def _():
        y_ref[...] = acc_ref[...].T


def _peer_dense(x1t, u_b, vt_b, e1, c1, e2, r2):
    d, n = x1t.shape
    ne = u_b.shape[0]
    heads, nk, _ = e1.shape
    tb = _pick(n, 512, 256, 128)
    te = PEER_TE
    nj = ne // te
    ti = te // nk
    sub = 2
    once = pl.Buffered(1)
    rspec = pl.BlockSpec((heads, nk, tb), lambda i, j: (0, 0, i), pipeline_mode=once)
    return pl.pallas_call(
        functools.partial(_peer_dense_kernel, ti=ti, sub=sub),
        out_shape=jax.ShapeDtypeStruct((n, d), F32),
        grid=(n // tb, nj + 1),
        in_specs=[pl.BlockSpec((d, tb), lambda i, j: (0, i), pipeline_mode=once),
                  pl.BlockSpec((te, d), lambda i, j: (jnp.minimum(j, nj - 1), 0)),
                  pl.BlockSpec((d, te), lambda i, j: (0, jnp.maximum(j - 1, 0))),
                  rspec, rspec, rspec, rspec],
        out_specs=pl.BlockSpec((tb, d), lambda i, j: (i, 0), pipeline_mode=once),
        scratch_shapes=[pltpu.VMEM((d, tb), F32), pltpu.VMEM((2, te, tb), BF16)],
        compiler_params=_cparams("parallel", "arbitrary"),
        name="peer_dense",
    )(x1t, u_b, vt_b, e1, c1, e2, r2)


def _cast_kernel(x_ref, o_ref):
    o_ref[...] = x_ref[...].astype(BF16)


def _cast_t_kernel(x_ref, o_ref):
    o_ref[...] = x_ref[...].T.astype(BF16)


def _cast_rows(x, transpose):
    r, d = x.shape
    tr = _pick(r, 512, 256, 128)
    if transpose:
        kern, oshape, ospec = _cast_t_kernel, (d, r), pl.BlockSpec((d, tr), lambda i: (0, i))
    else:
        kern, oshape, ospec = _cast_kernel, (r, d), pl.BlockSpec((tr, d), lambda i: (i, 0))
    return pl.pallas_call(
        kern,
        out_shape=jax.ShapeDtypeStruct(oshape, BF16),
        grid=(r // tr,),
        in_specs=[pl.BlockSpec((tr, d), lambda i: (i, 0))],
        out_specs=ospec,
        compiler_params=_cparams("parallel"),
        name="cast_t" if transpose else "cast",
    )(x)


def _final_kernel(xf_ref, xb_ref, ch_ref, p_ref, wg_ref, wp_ref, g_ref, b_ref, o_ref, acc0_ref, acc1_ref, *, tn,
                  nchunks):
    j = pl.program_id(1)

    def step(fill_ref, done_ref):
        cols = pl.ds(pl.multiple_of(j * tn, tn), tn)
        gate = _sigmoid(_dot(xb_ref[...], wg_ref[...]))
        proj = _dot(p_ref[...].astype(BF16), wp_ref[...])
        fill_ref[:, cols] = DEEPNORM_ALPHA * xf_ref[...] + ch_ref[...] + gate * proj

        def norm_chunk():
            rows = pl.ds(pl.multiple_of(j * LANES, LANES), LANES)
            o_ref[...] = _layer_norm(done_ref[rows, :], g_ref[...], b_ref[...])

        _lag_epilogue(j, nchunks, pl.num_programs(1), norm_chunk)

    _lag_run(acc0_ref, acc1_ref, step)


def _final(x1f, x1b, ch, p, wg_b, wp_b, g, b):
    n, d = x1f.shape
    pd = p.shape[1]
    tn = 1024
    ns = d // tn
    tm, mt, nchunks, cur, chunk = _lag_specs(n, ns)
    vec = pl.BlockSpec((1, d), lambda i, j: (0, 0))
    rows = lambda w: pl.BlockSpec((tm, w), lambda i, j: (cur(i), 0), pipeline_mode=pl.Buffered(1))
    tile = pl.BlockSpec((tm, tn), lambda i, j: (cur(i), j))
    return pl.pallas_call(
        functools.partial(_final_kernel, tn=tn, nchunks=nchunks),
        out_shape=jax.ShapeDtypeStruct((n, d), F32),
        grid=(mt + 1, ns),
        in_specs=[tile, rows(d), tile, rows(pd),
                  pl.BlockSpec((d, tn), lambda i, j: (0, j)),
                  pl.BlockSpec((pd, tn), lambda i, j: (0, j)),
                  vec, vec],
        out_specs=pl.BlockSpec((LANES, d), lambda i, j: (chunk(i, j), 0)),
        scratch_shapes=[pltpu.VMEM((tm, d), F32), pltpu.VMEM((tm, d), F32)],
        compiler_params=_cparams("arbitrary", "arbitrary"),
        name="ple_ln2",
    )(x1f, x1b, ch, p, wg_b, wp_b, g.reshape(1, d), b.reshape(1, d))


def _rope_tables(pos, half):
    inv = ROPE_BASE ** (-jnp.arange(half, dtype=F32) / half)
    ang = pos.astype(F32)[:, None] * inv[None]
    return jnp.cos(ang), jnp.sin(ang)


def _post_mixer(o_r, o_m, x_raw, p, wts):
    x1f, x1b, x1t = _out_proj_ln1(o_r, o_m, wts["w_out"], x_raw, wts["ln_emb_g"], wts["ln_emb_b"], wts["ln1_g"],
                                  wts["ln1_b"])
    heads = wts["keys"].shape[0] // 2
    scores = _peer_scores(x1b, wts["w_q"], wts["keys"])
    e1, rb, e2, b2 = _peer_route(scores, heads)
    ch = _peer_dense(x1t, wts["u"], wts["vt"], e1, rb, e2, b2)
    return _final(x1f, x1b, ch, p, wts["w_gate"], wts["w_proj"], wts["ln2_g"], wts["ln2_b"])


def kernel(x_prompt, x_sample, state_ret, state_conv, state_mlstm_c, state_mlstm_n, state_mlstm_m, p_prompt, p_sample,
           ln_emb_g, ln_emb_b, w_in, b_gate, conv_w, conv_b, g_ret_norm, g_ml_norm, w_out, ln1_g, ln1_b,
           w_peer_q, peer_sub_keys, peer_u, peer_v, w_ple_gate, w_ple_proj, ln2_g, ln2_b):
    bsz, t, d = x_prompt.shape
    nb = x_sample.shape[0]
    _, _, heads, dk, _ = state_ret.shape
    rw = heads * dk
    n_main = 8 * rw
    assert x_sample.shape[1] == 1 and w_in.shape[0] == DEPTH and t % CHUNK == 0

    w_in_t = jnp.swapaxes(w_in, 1, 2)
    wg_t = w_in_t[0, n_main:, :].astype(BF16)
    log_g = jnp.log1p(-(2.0 ** (-5.0 - jnp.arange(heads, dtype=F32))))
    keys = peer_sub_keys[0]
    wts = dict(
        ln_emb_g=ln_emb_g, ln_emb_b=ln_emb_b, ln1_g=ln1_g[0], ln1_b=ln1_b[0], ln2_g=ln2_g[0], ln2_b=ln2_b[0],
        w_out=w_out[0].astype(BF16), w_q=w_peer_q[0].astype(BF16),
        keys=keys.reshape(keys.shape[0] * 2, keys.shape[2], keys.shape[3]).astype(BF16),
        u=_cast_rows(peer_u[0], False), vt=_cast_rows(peer_v[0], True),
        w_gate=w_ple_gate[0].astype(BF16), w_proj=w_ple_proj[0].astype(BF16),
    )

    xp = x_prompt.reshape(bsz * t, d)
    xn = _ln_cast(xp, ln_emb_g, ln_emb_b)
    z = _in_proj(xn, w_in_t, n_main)
    gates_t = _gate_rows(xn, wg_t)
    cos, sin = _rope_tables(jnp.arange(t), dk // 2)
    o_r, ret_p = _ret_prompt(z, log_g, cos, sin, g_ret_norm[0], bsz, t, heads, dk, 0)
    o_m, c_p, n_p, m_p = _mlstm_prompt(z, gates_t, b_gate[0], conv_w[0], conv_b[0].reshape(1, 2 * rw), g_ml_norm[0],
                                       bsz, t, heads, dk, 4 * rw)
    conv_p = z.reshape(bsz, t, n_main)[:, t - (CONV_W - 1):, 4 * rw:6 * rw]
    y_prompt = _post_mixer(o_r, o_m, xp, p_prompt[0].reshape(bsz * t, -1), wts).reshape(bsz, t, d)

    xs = x_sample.reshape(nb, d)
    xns = _ln_cast(xs, ln_emb_g, ln_emb_b)
    zs = _in_proj(xns, w_in_t, n_main)
    gates_s = _gate_rows(xns, wg_t).T
    cs, sn = _rope_tables(jnp.full((1,), PAST_LEN), dk // 2)
    o_rs, o_ms, ret_s, conv_s, c_s, n_s, m_s = _mix_sample(
        zs, gates_s, log_g, b_gate[0], jnp.concatenate([cs, sn], axis=0), state_ret[0], state_conv[0],
        state_mlstm_c[0], state_mlstm_n[0], state_mlstm_m[0], conv_w[0], conv_b[0], g_ret_norm[0], g_ml_norm[0], heads, dk)
    y_sample = _post_mixer(o_rs, o_ms, xs, p_sample[0].reshape(nb, -1), wts).reshape(nb, 1, d)

    lead = lambda a: a[None]
    return (y_prompt, y_sample, lead(ret_p), lead(conv_p), lead(c_p), lead(n_p), lead(m_p),
            lead(ret_s), lead(conv_s), lead(c_s), lead(n_s), lead(m_s))
```

```python
import functools
import math

import jax
import jax.numpy as jnp
from jax import lax
from jax.experimental import pallas as pl
from jax.experimental.pallas import tpu as pltpu

F32 = jnp.float32
BF16 = jnp.bfloat16
I32 = jnp.int32

LN_EPS = 1e-5
DEPTH = 1
DEEPNORM_ALPHA = (2.0 * DEPTH) ** 0.25
CHUNK = 128
ROPE_BASE = 10000.0
PAST_LEN = 16384
PEER_TOPK = 16
PEER_TE = 512
CONV_W = 4

V7X_VMEM_LIMIT_BYTES = 56 * 1024 * 1024
LANES = 128


def _cparams(*sem):
    return pltpu.CompilerParams(dimension_semantics=sem, vmem_limit_bytes=V7X_VMEM_LIMIT_BYTES)


def _dot(a, b):
    return jnp.dot(a, b, preferred_element_type=F32)


def _dot_nt(a, b):
    return lax.dot_general(a, b, (((1,), (1,)), ((), ())), preferred_element_type=F32)


def _dot_tn(a, b):
    return lax.dot_general(a, b, (((0,), (0,)), ((), ())), preferred_element_type=F32)


def _sigmoid(x):
    return 1.0 / (1.0 + jnp.exp(-x))


def _layer_norm(x, g, b):
    mu = jnp.mean(x, axis=-1, keepdims=True)
    xc = x - mu
    var = jnp.mean(xc * xc, axis=-1, keepdims=True)
    return xc * lax.rsqrt(var + LN_EPS) * g + b


def _head_norm(x, g):
    mu = jnp.mean(x, axis=-1, keepdims=True)
    xc = x - mu
    var = jnp.mean(xc * xc, axis=-1, keepdims=True)
    return xc * lax.rsqrt(var + LN_EPS) * g


def _pick(n, *cands):
    for c in cands:
        if n % c == 0:
            return c
    return n


def _ln_cast_kernel(x_ref, g_ref, b_ref, o_ref):
    o_ref[...] = _layer_norm(x_ref[...], g_ref[...], b_ref[...]).astype(BF16)


def _ln_cast(x, g, b):
    n, d = x.shape
    tm = _pick(n, 256, 128)
    return pl.pallas_call(
        _ln_cast_kernel,
        out_shape=jax.ShapeDtypeStruct((n, d), BF16),
        grid=(n // tm,),
        in_specs=[pl.BlockSpec((tm, d), lambda i: (i, 0)),
                  pl.BlockSpec((1, d), lambda i: (0, 0)),
                  pl.BlockSpec((1, d), lambda i: (0, 0))],
        out_specs=pl.BlockSpec((tm, d), lambda i: (i, 0)),
        compiler_params=_cparams("parallel"),
        name="ln_cast",
    )(x, g.reshape(1, d), b.reshape(1, d))


def _in_proj_kernel(x_ref, w_ref, o_ref, wb_ref):
    @pl.when(pl.program_id(1) == 0)
    def _():
        wb_ref[...] = w_ref[...].astype(BF16)

    o_ref[...] = _dot_nt(x_ref[...], wb_ref[...])


def _in_proj(xn, w_in_t, n_main):
    n, d = xn.shape
    tm = _pick(n, 1024, 512, 256, 128)
    tn = 512
    return pl.pallas_call(
        _in_proj_kernel,
        out_shape=jax.ShapeDtypeStruct((n, n_main), F32),
        grid=(n_main // tn, n // tm),
        in_specs=[pl.BlockSpec((tm, d), lambda j, i: (i, 0)),
                  pl.BlockSpec((None, tn, d), lambda j, i: (0, j, 0))],
        out_specs=pl.BlockSpec((tm, tn), lambda j, i: (i, j)),
        scratch_shapes=[pltpu.VMEM((tn, d), BF16)],
        compiler_params=_cparams("parallel", "arbitrary"),
        name="in_proj",
    )(xn, w_in_t)


def _gate_rows_kernel(w_ref, x_ref, o_ref):
    o_ref[...] = _dot_nt(w_ref[...], x_ref[...])


def _gate_rows(xn, wg_t):
    n, d = xn.shape
    g = wg_t.shape[0]
    tb = _pick(n, 1024, 512, 256, 128)
    return pl.pallas_call(
        _gate_rows_kernel,
        out_shape=jax.ShapeDtypeStruct((g, n), F32),
        grid=(n // tb,),
        in_specs=[pl.BlockSpec((g, d), lambda i: (0, 0)),
                  pl.BlockSpec((tb, d), lambda i: (i, 0))],
        out_specs=pl.BlockSpec((g, tb), lambda i: (0, i)),
        compiler_params=_cparams("parallel"),
        name="gate_rows",
    )(wg_t, xn)


def _rope(x, cos, sin):
    half = x.shape[-1] // 2
    x1, x2 = x[:, :half], x[:, half:]
    return jnp.concatenate([x1 * cos - x2 * sin, x1 * sin + x2 * cos], axis=-1)


def _log_sigmoid(x):
    return jnp.minimum(x, 0.0) - jnp.log1p(jnp.exp(-jnp.abs(x)))


def _row_to_col(row, eye):
    return jnp.sum(jnp.where(eye, row, 0.0), axis=1, keepdims=True)


HEAD_GROUP = 2


def _ret_prompt_kernel(lg_ref, q_ref, k_ref, v_ref, g_ref, cos_ref, sin_ref, gn_ref, o_ref, s_ref, *, chunk, dk):
    grp = pl.program_id(1)
    L = chunk
    t = q_ref.shape[0]
    hg = q_ref.shape[1] // dk
    ii = lax.broadcasted_iota(I32, (L, L), 0)
    jj = lax.broadcasted_iota(I32, (L, L), 1)
    causal = ii >= jj
    diff = jnp.where(causal, (ii - jj).astype(F32), 0.0)
    idx = lax.broadcasted_iota(I32, (L, 1), 0).astype(F32)
    scale = dk ** -0.5
    s_ref[...] = jnp.zeros_like(s_ref)
    per_head = []
    for u in range(hg):
        lg = lg_ref[grp * hg + u]
        per_head.append(dict(
            decay_in=jnp.where(causal, jnp.exp(lg * diff), 0.0),
            decay_q=jnp.exp(lg * (idx + 1.0)),
            decay_k=jnp.exp(lg * (float(L) - 1.0 - idx)),
            decay_c=jnp.exp(lg * jnp.full((1, 1), float(L), F32)),
            gn=gn_ref[pl.ds(grp * hg + u, 1), :]))

    def body(c, carry):
        rows = pl.ds(pl.multiple_of(c * L, L), L)
        cos, sin = cos_ref[rows, :], sin_ref[rows, :]
        for u, hd in enumerate(per_head):
            cols = slice(u * dk, (u + 1) * dk)
            rq = _rope(q_ref[rows, cols], cos, sin)
            rk = _rope(k_ref[rows, cols], cos, sin) * scale
            vb = v_ref[rows, cols].astype(BF16)
            rqb = rq.astype(BF16)
            s = s_ref[u]
            sc = _dot_nt(rqb, rk.astype(BF16)) * hd["decay_in"]
            o = _dot(sc.astype(BF16), vb) + _dot(rqb, s.astype(BF16)) * hd["decay_q"]
            s_ref[u] = s * hd["decay_c"] + _dot_tn((rk * hd["decay_k"]).astype(BF16), vb)
            g = g_ref[rows, cols]
            o_ref[rows, cols] = (_head_norm(o, hd["gn"]) * (g * _sigmoid(g))).astype(o_ref.dtype)
        return carry

    lax.fori_loop(0, t // L, body, 0)


def _ret_prompt(z, log_g, cos, sin, g_norm, bsz, t, heads, dk, col0):
    hg = HEAD_GROUP if heads % HEAD_GROUP == 0 else 1
    ng = heads // hg
    gb = col0 // (hg * dk)
    zspec = lambda off: pl.BlockSpec((t, hg * dk), lambda b, g, *_: (b, gb + off * ng + g))
    grid_spec = pltpu.PrefetchScalarGridSpec(
        num_scalar_prefetch=1,
        grid=(bsz, ng),
        in_specs=[zspec(0), zspec(1), zspec(2), zspec(3),
                  pl.BlockSpec((t, dk // 2), lambda b, g, *_: (0, 0)),
                  pl.BlockSpec((t, dk // 2), lambda b, g, *_: (0, 0)),
                  pl.BlockSpec((heads, dk), lambda b, g, *_: (0, 0))],
        out_specs=[pl.BlockSpec((t, hg * dk), lambda b, g, *_: (b, g)),
                   pl.BlockSpec((None, hg, dk, dk), lambda b, g, *_: (b, g, 0, 0))],
    )
    return pl.pallas_call(
        functools.partial(_ret_prompt_kernel, chunk=CHUNK, dk=dk),
        out_shape=[jax.ShapeDtypeStruct((bsz * t, heads * dk), BF16),
                   jax.ShapeDtypeStruct((bsz, heads, dk, dk), F32)],
        grid_spec=grid_spec,
        compiler_params=_cparams("parallel", "parallel"),
        name="ret_prompt",
    )(log_g, z, z, z, z, cos, sin, g_norm)


def _mlstm_prompt_kernel(bg_ref, xq_ref, xk_ref, v_ref, og_ref, ig_ref, fg_ref, cwq_ref, cwk_ref, cbq_ref, cbk_ref,
                         gn_ref, o_ref, c_ref, n_ref, m_ref, bt_s, ig_s, *, chunk, heads, dk):
    grp = pl.program_id(1)
    L = chunk
    t = xq_ref.shape[0]
    hg = xq_ref.shape[1] // dk
    nc = t // L
    scale = dk ** -0.5

    lane = lax.broadcasted_iota(I32, (nc, L), 1)
    for u in range(hg):
        h = grp * hg + u
        ig_s[u] = ig_ref[u] + bg_ref[h]
        bt = _log_sigmoid(fg_ref[u] + bg_ref[heads + h])
        s = 1
        while s < L:
            bt = bt + jnp.where(lane >= s, pltpu.roll(bt, s, axis=1), 0.0)
            s *= 2
        bt_s[u] = bt

    ii = lax.broadcasted_iota(I32, (L, L), 0)
    jj = lax.broadcasted_iota(I32, (L, L), 1)
    causal = ii >= jj
    eye = ii == jj
    row = lax.broadcasted_iota(I32, (L, 1), 0)
    gns = [gn_ref[pl.ds(grp * hg + u, 1), :] for u in range(hg)]
    c_ref[...] = jnp.zeros_like(c_ref)
    n_ref[...] = jnp.zeros_like(n_ref)

    def conv_silu(x_ref, w_ref, b_ref, c, rows, cols):
        x = x_ref[rows, cols]
        prev_rows = pl.ds(pl.multiple_of(jnp.maximum(c - 1, 0) * L, L), L)
        xp = jnp.where(c > 0, x_ref[prev_rows, cols], 0.0)
        y = x * w_ref[CONV_W - 1:CONV_W, cols] + b_ref[:, cols]
        for j in range(1, CONV_W):
            xs = jnp.where(row < j, pltpu.roll(xp, j, axis=0), pltpu.roll(x, j, axis=0))
            y = y + xs * w_ref[CONV_W - 1 - j:CONV_W - j, cols]
        return y * _sigmoid(y)

    def head_step(u, c, rows, m):
        cols = slice(u * dk, (u + 1) * dk)
        q = conv_silu(xq_ref, cwq_ref, cbq_ref, c, rows, cols)
        k = conv_silu(xk_ref, cwk_ref, cbk_ref, c, rows, cols) * scale
        vb = v_ref[rows, cols].astype(BF16)
        qb = q.astype(BF16)
        bt_row = bt_s[u, pl.ds(c, 1), :]
        ig_row = ig_s[u, pl.ds(c, 1), :]
        bt_col = _row_to_col(bt_row, eye)
        ig_col = _row_to_col(ig_row, eye)
        dmat = jnp.where(causal, bt_col - bt_row + ig_row, -jnp.inf)
        prior = bt_col + m
        mt = jnp.maximum(prior, jnp.max(dmat, axis=1, keepdims=True))
        w = jnp.exp(dmat - mt)
        wp = jnp.exp(prior - mt)
        qk = _dot_nt(qb, k.astype(BF16)) * w
        cst = c_ref[u]
        nst = n_ref[u]
        num = _dot(qk.astype(BF16), vb) + _dot(qb, cst.astype(BF16)) * wp
        den = jnp.sum(qk, axis=1, keepdims=True) + jnp.sum(q * nst, axis=1, keepdims=True) * wp
        hh = num / jnp.maximum(jnp.abs(den), jnp.exp(-mt))
        bl = bt_row[:, L - 1:L]
        m_new = mt[L - 1:L, :]
        wk = jnp.exp(bl - bt_col + ig_col - m_new)
        wc = jnp.exp(bl + m - m_new)
        kw = k * wk
        c_ref[u] = cst * wc + _dot_tn(kw.astype(BF16), vb)
        n_ref[u] = nst * wc + jnp.sum(kw, axis=0, keepdims=True)
        og = og_ref[rows, cols]
        o_ref[rows, cols] = (_head_norm(hh, gns[u]) * _sigmoid(og)).astype(o_ref.dtype)
        return m_new

    def body(c, ms):
        rows = pl.ds(pl.multiple_of(c * L, L), L)
        return tuple(head_step(u, c, rows, ms[u]) for u in range(hg))

    m_fin = lax.fori_loop(0, nc, body, tuple(jnp.zeros((1, 1), F32) for _ in range(hg)))
    for u in range(hg):
        m_ref[u] = jnp.broadcast_to(m_fin[u], m_ref.shape[1:])


def _mlstm_prompt(z, gates_t, b_gate, conv_w, conv_b, g_norm, bsz, t, heads, dk, col0):
    hg = HEAD_GROUP if heads % HEAD_GROUP == 0 else 1
    ng = heads // hg
    gb = col0 // (hg * dk)
    nc = t // CHUNK
    zspec = lambda off: pl.BlockSpec((t, hg * dk), lambda b, g, *_: (b, gb + off * ng + g))
    gates4 = gates_t.reshape(2 * heads, bsz, nc, CHUNK)
    gspec = lambda off: pl.BlockSpec((hg, None, nc, CHUNK), lambda b, g, *_: (off * ng + g, b, 0, 0))
    wspec = lambda rows, off: pl.BlockSpec((rows, hg * dk), lambda b, g, *_: (0, off * ng + g))
    grid_spec = pltpu.PrefetchScalarGridSpec(
        num_scalar_prefetch=1,
        grid=(bsz, ng),
        in_specs=[zspec(0), zspec(1), zspec(2), zspec(3), gspec(0), gspec(1),
                  wspec(CONV_W, 0), wspec(CONV_W, 1), wspec(1, 0), wspec(1, 1),
                  pl.BlockSpec((heads, dk), lambda b, g, *_: (0, 0))],
        out_specs=[pl.BlockSpec((t, hg * dk), lambda b, g, *_: (b, g)),
                   pl.BlockSpec((None, hg, dk, dk), lambda b, g, *_: (b, g, 0, 0)),
                   pl.BlockSpec((None, hg, 1, dk), lambda b, g, *_: (b, g, 0, 0)),
                   pl.BlockSpec((None, hg, 1, LANES), lambda b, g, *_: (b, g, 0, 0))],
        scratch_shapes=[pltpu.VMEM((hg, nc, CHUNK), F32), pltpu.VMEM((hg, nc, CHUNK), F32)],
    )
    o, c, n, m = pl.pallas_call(
        functools.partial(_mlstm_prompt_kernel, chunk=CHUNK, heads=heads, dk=dk),
        out_shape=[jax.ShapeDtypeStruct((bsz * t, heads * dk), BF16),
                   jax.ShapeDtypeStruct((bsz, heads, dk, dk), F32),
                   jax.ShapeDtypeStruct((bsz, heads, 1, dk), F32),
                   jax.ShapeDtypeStruct((bsz, heads, 1, LANES), F32)],
        grid_spec=grid_spec,
        compiler_params=_cparams("parallel", "parallel"),
        name="mlstm_prompt",
    )(b_gate, z, z, z, z, gates4, gates4, conv_w, conv_w, conv_b, conv_b, g_norm)
    return o, c, n[:, :, 0, :], m[:, :, 0, 0]


def _pad_rows(row, rows=8):
    r = lax.broadcasted_iota(I32, (rows, row.shape[1]), 0)
    return jnp.where(r == 0, row, 0.0)


SEQ_GROUP = 2


def _mix_sample_kernel(lg_ref, bg_ref, z_ref, gt_ref, cs_ref, sr_ref, cb_ref, cc_ref, cn_ref, cm_ref, cw_ref, cbias_ref,
                       gr_ref, gm_ref, or_ref, om_ref, sro_ref, cbo_ref, cco_ref, cno_ref, cmo_ref, *, heads, dk):
    for s in range(z_ref.shape[0]):
        _mix_one(lg_ref, bg_ref, z_ref.at[s], gt_ref.at[s], cs_ref, sr_ref.at[s], cb_ref.at[s], cc_ref.at[s],
                 cn_ref.at[s], cm_ref.at[s], cw_ref, cbias_ref, gr_ref, gm_ref, or_ref.at[s], om_ref.at[s],
                 sro_ref.at[s], cbo_ref.at[s], cco_ref.at[s], cno_ref.at[s], cmo_ref.at[s], heads=heads, dk=dk)


def _mix_one(lg_ref, bg_ref, z_ref, gt_ref, cs_ref, sr_ref, cb_ref, cc_ref, cn_ref, cm_ref, cw_ref, cbias_ref,
             gr_ref, gm_ref, or_ref, om_ref, sro_ref, cbo_ref, cco_ref, cno_ref, cmo_ref, *, heads, dk):
    rw = heads * dk
    cos, sin = cs_ref[0:1, :], cs_ref[1:2, :]
    scale = dk ** -0.5
    xqk = z_ref[:, 4 * rw:6 * rw]
    buf = cb_ref[...]
    y = xqk * cw_ref[CONV_W - 1:CONV_W, :] + cbias_ref[...]
    for j in range(CONV_W - 1):
        y = y + buf[j:j + 1, :] * cw_ref[j:j + 1, :]
    qk_act = y * _sigmoid(y)
    cbo_ref[0:CONV_W - 2, :] = buf[1:CONV_W - 1, :]
    cbo_ref[CONV_W - 2:CONV_W - 1, :] = xqk

    for h in range(heads):
        sl = lambda g: slice(g * rw + h * dk, g * rw + (h + 1) * dk)
        gam = jnp.exp(jnp.full((1, 1), lg_ref[h], F32))
        rq = _rope(z_ref[:, sl(0)], cos, sin)
        rk = _rope(z_ref[:, sl(1)], cos, sin) * scale
        v = z_ref[:, sl(2)]
        rg = z_ref[:, sl(3)]
        s = sr_ref[h]
        sc = jnp.sum(rq * rk, axis=1, keepdims=True)
        qs = _dot(jnp.broadcast_to(rq, (8, dk)).astype(BF16), s.astype(BF16))[0:1, :]
        o = sc * v + qs * gam
        sro_ref[h] = s * gam + _dot_tn(_pad_rows(rk).astype(BF16), jnp.broadcast_to(v, (8, dk)).astype(BF16))
        or_ref[:, h * dk:(h + 1) * dk] = (_head_norm(o, gr_ref[h:h + 1, :]) * (rg * _sigmoid(rg))).astype(or_ref.dtype)
        q = qk_act[:, h * dk:(h + 1) * dk]
        k = qk_act[:, rw + h * dk:rw + (h + 1) * dk] * scale
        v = z_ref[:, sl(6)]
        og = z_ref[:, sl(7)]
        it = gt_ref[:, h:h + 1] + bg_ref[h]
        lf = _log_sigmoid(gt_ref[:, heads + h:heads + h + 1] + bg_ref[heads + h])
        m = cm_ref[:, h:h + 1]
        cst = cc_ref[h]
        nst = cn_ref[h:h + 1, :]
        prior = lf + m
        mt = jnp.maximum(prior, it)
        w = jnp.exp(it - mt)
        wp = jnp.exp(prior - mt)
        qk = jnp.sum(q * k, axis=1, keepdims=True) * w
        qc = _dot(jnp.broadcast_to(q, (8, dk)).astype(BF16), cst.astype(BF16))[0:1, :]
        num = qk * v + qc * wp
        den = qk + jnp.sum(q * nst, axis=1, keepdims=True) * wp
        hh = num / jnp.maximum(jnp.abs(den), jnp.exp(-mt))
        wk = jnp.exp(it - mt)
        wc = jnp.exp(lf + m - mt)
        kw = k * wk
        cco_ref[h] = cst * wc + _dot_tn(_pad_rows(kw).astype(BF16), jnp.broadcast_to(v, (8, dk)).astype(BF16))
        cno_ref[h:h + 1, :] = nst * wc + kw
        cmo_ref[:, h:h + 1] = mt
        om_ref[:, h * dk:(h + 1) * dk] = (_head_norm(hh, gm_ref[h:h + 1, :]) * _sigmoid(og)).astype(om_ref.dtype)


def _mix_sample(z, gates, log_g, b_gate, cos_sin, s_ret, s_conv, s_c, s_n, s_m, conv_w, conv_b, g_ret, g_ml, heads, dk):
    nb = z.shape[0]
    rw = heads * dk
    sg = SEQ_GROUP if nb % SEQ_GROUP == 0 else 1
    per_b3 = lambda *tail: pl.BlockSpec((sg,) + tail, lambda b, *_: (b,) + (0,) * len(tail))
    whole = lambda a: pl.BlockSpec(a.shape, lambda b, *_: (0,) * a.ndim)
    z3 = z.reshape(nb, 1, 8 * rw)
    g3 = gates.reshape(nb, 1, 2 * heads)
    m3 = s_m.reshape(nb, 1, heads)
    cb2 = conv_b.reshape(1, 2 * rw)
    grid_spec = pltpu.PrefetchScalarGridSpec(
        num_scalar_prefetch=2,
        grid=(nb // sg,),
        in_specs=[per_b3(1, 8 * rw), per_b3(1, 2 * heads), whole(cos_sin),
                  per_b3(heads, dk, dk), per_b3(CONV_W - 1, 2 * rw), per_b3(heads, dk, dk), per_b3(heads, dk),
                  per_b3(1, heads), whole(conv_w), whole(cb2), whole(g_ret), whole(g_ml)],
        out_specs=[per_b3(1, rw), per_b3(1, rw), per_b3(heads, dk, dk), per_b3(CONV_W - 1, 2 * rw),
                   per_b3(heads, dk, dk), per_b3(heads, dk), per_b3(1, heads)],
    )
    o_r, o_m, sr, cb, cc, cn, cm = pl.pallas_call(
        functools.partial(_mix_sample_kernel, heads=heads, dk=dk),
        out_shape=[jax.ShapeDtypeStruct((nb, 1, rw), BF16), jax.ShapeDtypeStruct((nb, 1, rw), BF16),
                   jax.ShapeDtypeStruct(s_ret.shape, F32), jax.ShapeDtypeStruct(s_conv.shape, F32),
                   jax.ShapeDtypeStruct(s_c.shape, F32), jax.ShapeDtypeStruct(s_n.shape, F32),
                   jax.ShapeDtypeStruct((nb, 1, heads), F32)],
        grid_spec=grid_spec,
        compiler_params=_cparams("parallel"),
        name="mix_sample",
    )(log_g, b_gate, z3, g3, cos_sin, s_ret, s_conv, s_c, s_n, m3, conv_w, cb2, g_ret, g_ml)
    return o_r.reshape(nb, rw), o_m.reshape(nb, rw), sr, cb, cc, cn, cm.reshape(nb, heads)


def _lag_specs(n, ns):
    tm = _pick(n, *(c for c in (512, 256, 128) if c // LANES <= ns))
    mt = n // tm
    nchunks = tm // LANES
    cur = lambda i: jnp.minimum(i, mt - 1)
    chunk = lambda i, j: jnp.maximum(i - 1, 0) * nchunks + jnp.where(i == 0, 0, jnp.minimum(j, nchunks - 1))
    return tm, mt, nchunks, cur, chunk


def _lag_epilogue(j, nchunks, ns, fn):
    if nchunks == ns:
        fn()
    else:
        pl.when(j < nchunks)(fn)


def _lag_run(acc0_ref, acc1_ref, step):
    i, j = pl.program_id(0), pl.program_id(1)

    @pl.when((i == 0) & (j == 0))
    def _():
        acc1_ref[...] = jnp.zeros_like(acc1_ref)

    @pl.when(i % 2 == 0)
    def _():
        step(acc0_ref, acc1_ref)

    @pl.when(i % 2 == 1)
    def _():
        step(acc1_ref, acc0_ref)


def _out_proj_kernel(a_ref, b_ref, wa_ref, wb_ref, x_ref, eg_ref, eb_ref, g_ref, bb_ref, of_ref, ob_ref, ot_ref,
                     acc0_ref, acc1_ref, *, tn, nchunks):
    j = pl.program_id(1)

    def step(fill_ref, done_ref):
        cols = pl.ds(pl.multiple_of(j * tn, tn), tn)
        fill_ref[:, cols] = _dot(a_ref[...], wa_ref[...]) + _dot(b_ref[...], wb_ref[...])

        def norm_chunk():
            rows = pl.ds(pl.multiple_of(j * LANES, LANES), LANES)
            xe = _layer_norm(x_ref[...], eg_ref[...], eb_ref[...])
            x1 = _layer_norm(DEEPNORM_ALPHA * xe + done_ref[rows, :], g_ref[...], bb_ref[...])
            of_ref[...] = x1
            ob_ref[...] = x1.astype(BF16)
            ot_ref[...] = x1.T.astype(BF16)

        _lag_epilogue(j, nchunks, pl.num_programs(1), norm_chunk)

    _lag_run(acc0_ref, acc1_ref, step)


def _out_proj_ln1(o_r, o_m, w_out_b, x_raw, eg, eb, g, b):
    n, d = x_raw.shape
    ka = o_r.shape[1]
    tn = 1024
    ns = d // tn
    tm, mt, nchunks, cur, chunk = _lag_specs(n, ns)
    row = lambda a: a.reshape(1, d)
    vec = pl.BlockSpec((1, d), lambda i, j: (0, 0))
    return pl.pallas_call(
        functools.partial(_out_proj_kernel, tn=tn, nchunks=nchunks),
        out_shape=[jax.ShapeDtypeStruct((n, d), F32), jax.ShapeDtypeStruct((n, d), BF16),
                   jax.ShapeDtypeStruct((d, n), BF16)],
        grid=(mt + 1, ns),
        in_specs=[pl.BlockSpec((tm, ka), lambda i, j: (cur(i), 0), pipeline_mode=pl.Buffered(1)),
                  pl.BlockSpec((tm, ka), lambda i, j: (cur(i), 0), pipeline_mode=pl.Buffered(1)),
                  pl.BlockSpec((ka, tn), lambda i, j: (0, j)),
                  pl.BlockSpec((ka, tn), lambda i, j: (1, j)),
                  pl.BlockSpec((LANES, d), lambda i, j: (chunk(i, j), 0)),
                  vec, vec, vec, vec],
        out_specs=[pl.BlockSpec((LANES, d), lambda i, j: (chunk(i, j), 0)),
                   pl.BlockSpec((LANES, d), lambda i, j: (chunk(i, j), 0)),
                   pl.BlockSpec((d, LANES), lambda i, j: (0, chunk(i, j)))],
        scratch_shapes=[pltpu.VMEM((tm, d), F32), pltpu.VMEM((tm, d), F32)],
        compiler_params=_cparams("arbitrary", "arbitrary"),
        name="out_proj_ln1",
    )(o_r, o_m, w_out_b, w_out_b, x_raw, row(eg), row(eb), row(g), row(b))


def _peer_scores_kernel(x_ref, wq_ref, keys_ref, o_ref):
    q = _dot(x_ref[...], wq_ref[...])
    nk, kd = keys_ref.shape[1], keys_ref.shape[2]
    for hp in range(keys_ref.shape[0]):
        qh = q[:, hp * kd:(hp + 1) * kd].astype(BF16)
        o_ref[hp] = _dot_nt(keys_ref[hp], qh)


def _peer_scores(x1b, wq_b, keys_b):
    n, d = x1b.shape
    hp, nk, kd = keys_b.shape
    tb = _pick(n, 512, 256, 128)
    return pl.pallas_call(
        _peer_scores_kernel,
        out_shape=jax.ShapeDtypeStruct((hp, nk, n), F32),
        grid=(n // tb,),
        in_specs=[pl.BlockSpec((tb, d), lambda i: (i, 0)),
                  pl.BlockSpec(wq_b.shape, lambda i: (0, 0), pipeline_mode=pl.Buffered(1)),
                  pl.BlockSpec(keys_b.shape, lambda i: (0, 0, 0))],
        out_specs=pl.BlockSpec((hp, nk, tb), lambda i: (0, 0, i)),
        compiler_params=_cparams("parallel"),
        name="peer_scores",
    )(x1b, wq_b, keys_b)


def _top_ranks(s, k):
    n, tb = s.shape
    idx = lax.broadcasted_iota(I32, (n, tb), 0)
    rank = jnp.full((n, tb), k, I32)
    vals = []
    for r in range(k):
        m = jnp.max(s, axis=0, keepdims=True)
        first = jnp.min(jnp.where(s == m, idx, n), axis=0, keepdims=True)
        hit = idx == first
        rank = jnp.where(hit, r, rank)
        s = jnp.where(hit, -jnp.inf, s)
        vals.append(m)
    return jnp.concatenate(vals, axis=0), rank


def _route_head_exact(s1, s2, topk):
    a, r1 = _top_ranks(s1, topk)
    b, r2 = _top_ranks(s2, topk)
    cand = jnp.concatenate([a[r:r + 1, :] + b for r in range(topk)], axis=0)
    tv, rc = _top_ranks(cand, topk)
    z = jnp.sum(jnp.exp(tv - tv[0:1, :]), axis=0, keepdims=True)
    c1 = jnp.zeros(s1.shape, I32)
    for r in range(topk):
        sel = rc[r * topk:(r + 1) * topk, :] < topk
        c1 = jnp.where(r1 == r, jnp.sum(sel.astype(I32), axis=0, keepdims=True), c1)
    return jnp.exp(s1 - a[0:1, :]) / z, c1, jnp.exp(s2 - b[0:1, :]), r2


def _top_values(s, k, on_hit):
    vals = []
    for r in range(k):
        m = jnp.max(s, axis=0, keepdims=True)
        hit = s == m
        on_hit(r, hit)
        s = jnp.where(hit, -jnp.inf, s)
        vals.append(m)
    return vals


def _route_head_fast(s1, s2, topk):
    n, tb = s1.shape
    sub = 8
    assert topk == 2 * sub
    st = dict(r1=jnp.full((n, tb), topk, I32), r2=jnp.full((n, tb), topk, I32))

    def hit1(r, hit):
        st["r1"] = jnp.where(hit, r, st["r1"])

    def hit2(r, hit):
        st["r2"] = jnp.where(hit, r, st["r2"])

    a = _top_values(s1, topk, hit1)
    b = _top_values(s2, topk, hit2)
    r1, r2 = st["r1"], st["r2"]
    a_arr = jnp.concatenate(a, axis=0)
    b_arr = jnp.concatenate(b, axis=0)
    row = lax.broadcasted_iota(I32, (sub, tb), 0)
    slabs, cols = [], []
    for q2 in range(sub):
        lim = topk // (q2 + 1)
        for r1s in range(0, lim, sub):
            slab = a_arr[r1s:r1s + sub, :] + b_arr[q2:q2 + 1, :]
            if lim - r1s < sub:
                slab = jnp.where(row < lim - r1s, slab, -jnp.inf)
            slabs.append(slab)
            cols.append((r1s, q2))
    slabs.append(a_arr[0:1, :] + b_arr[sub:topk, :])
    cand = jnp.concatenate(slabs, axis=0)
    st["sel"] = jnp.zeros(cand.shape, I32)

    def hit3(r, hit):
        st["sel"] = jnp.where(hit, 1, st["sel"])

    tv = _top_values(cand, topk, hit3)
    sel = st["sel"]
    count = lambda m: jnp.sum(m.astype(I32), axis=0, keepdims=True)
    rowcount = [jnp.zeros((sub, tb), I32) for _ in range(topk // sub)]
    for i, (r1s, _) in enumerate(cols):
        rowcount[r1s // sub] = rowcount[r1s // sub] + sel[i * sub:(i + 1) * sub, :]
    rowcount[0] = rowcount[0] + jnp.where(row == 0, count(sel[len(cols) * sub:, :]), 0)
    ok = (count(r1 < topk) == topk) & (count(r2 < topk) == topk) & (count(sel) == topk)
    z = jnp.ones_like(tv[0])
    for r in range(1, topk):
        z = z + jnp.exp(tv[r] - tv[0])
    c1 = jnp.zeros((n, tb), I32)
    for r in range(topk):
        c1 = jnp.where(r1 == r, rowcount[r // sub][r % sub:r % sub + 1, :], c1)
    return jnp.exp(s1 - a[0]) / z, c1, jnp.exp(s2 - b[0]), r2, ok


def _peer_route_kernel(s_ref, e1_ref, c1_ref, e2_ref, r2_ref, *, topk):
    heads = e1_ref.shape[0]
    tb = s_ref.shape[2]
    bad = jnp.zeros((1, tb), I32)
    for h in range(heads):
        e1, c1, e2, r2, ok = _route_head_fast(s_ref[2 * h], s_ref[2 * h + 1], topk)
        e1_ref[h], c1_ref[h], e2_ref[h], r2_ref[h] = e1, c1, e2, r2
        bad = jnp.where(ok, bad, 1)

    @pl.when(jnp.max(bad) > 0)
    def _():
        def redo(h, carry):
            e1, c1, e2, r2 = _route_head_exact(s_ref[2 * h], s_ref[2 * h + 1], topk)
            e1_ref[h], c1_ref[h], e2_ref[h], r2_ref[h] = e1, c1, e2, r2
            return carry

        lax.fori_loop(0, heads, redo, 0)


def _peer_route(scores, heads):
    hp, nk, n = scores.shape
    tb = LANES
    spec = pl.BlockSpec((heads, nk, tb), lambda i: (0, 0, i))
    return pl.pallas_call(
        functools.partial(_peer_route_kernel, topk=PEER_TOPK),
        out_shape=[jax.ShapeDtypeStruct((heads, nk, n), F32), jax.ShapeDtypeStruct((heads, nk, n), I32),
                   jax.ShapeDtypeStruct((heads, nk, n), F32), jax.ShapeDtypeStruct((heads, nk, n), I32)],
        grid=(n // tb,),
        in_specs=[pl.BlockSpec((hp, nk, tb), lambda i: (0, 0, i))],
        out_specs=[spec, spec, spec, spec],
        compiler_params=_cparams("parallel"),
        name="peer_route",
    )(scores)


def _gelu_tanh(x):
    c = math.sqrt(2.0 / math.pi)
    return x * (0.5 * (1.0 + jnp.tanh(c * (x + 0.044715 * (x * x * x)))))


def _peer_dense_kernel(xt_ref, u_ref, vt_ref, e1_ref, c1_ref, e2_ref, r2_ref, y_ref, acc_ref, a_ref, *, ti, sub):
    j = pl.program_id(1)
    nj = pl.num_programs(1) - 1
    heads, nk, tb = e2_ref.shape

    @pl.when(j == 0)
    def _():
        acc_ref[...] = jnp.zeros_like(acc_ref)
        a_ref[1] = jnp.zeros(a_ref.shape[1:], a_ref.dtype)

    jc = jnp.minimum(j, nj - 1)
    tiles = [slice(s * sub * nk, (s + 1) * sub * nk) for s in range(ti // sub)]
    acts = [_dot(u_ref[rows, :], xt_ref[...]) for rows in tiles]
    acc_ref[...] += _dot(vt_ref[...], a_ref[(j + 1) % 2])
    for s, rows in enumerate(tiles):
        act = _gelu_tanh(acts[s])
        parts = []
        for t in range(sub):
            i1 = pl.ds(jc * ti + s * sub + t, 1)
            g = jnp.zeros((nk, tb), F32)
            for h in range(heads):
                hit = r2_ref[h] < c1_ref[h, i1, :]
                g = g + jnp.where(hit, e1_ref[h, i1, :] * e2_ref[h], 0.0)
            parts.append((act[t * nk:(t + 1) * nk, :] * g).astype(BF16))
        a_ref[j % 2, rows, :] = jnp.concatenate(parts, axis=0)

    @pl.when(j == nj)
    def _():
        y_ref[...] = acc_ref[...].T


def _peer_dense(x1t, u_b, vt_b, e1, c1, e2, r2):
    d, n = x1t.shape
    ne = u_b.shape[0]
    heads, nk, _ = e1.shape
    tb = _pick(n, 512, 256, 128)
    te = PEER_TE
    nj = ne // te
    ti = te // nk
    sub = 2
    once = pl.Buffered(1)
    rspec = pl.BlockSpec((heads, nk, tb), lambda i, j: (0, 0, i), pipeline_mode=once)
    return pl.pallas_call(
        functools.partial(_peer_dense_kernel, ti=ti, sub=sub),
        out_shape=jax.ShapeDtypeStruct((n, d), F32),
        grid=(n // tb, nj + 1),
        in_specs=[pl.BlockSpec((d, tb), lambda i, j: (0, i), pipeline_mode=once),
                  pl.BlockSpec((te, d), lambda i, j: (jnp.minimum(j, nj - 1), 0)),
                  pl.BlockSpec((d, te), lambda i, j: (0, jnp.maximum(j - 1, 0))),
                  rspec, rspec, rspec, rspec],
        out_specs=pl.BlockSpec((tb, d), lambda i, j: (i, 0), pipeline_mode=once),
        scratch_shapes=[pltpu.VMEM((d, tb), F32), pltpu.VMEM((2, te, tb), BF16)],
        compiler_params=_cparams("parallel", "arbitrary"),
        name="peer_dense",
    )(x1t, u_b, vt_b, e1, c1, e2, r2)


def _cast_kernel(x_ref, o_ref):
    o_ref[...] = x_ref[...].astype(BF16)


def _cast_t_kernel(x_ref, o_ref):
    o_ref[...] = x_ref[...].T.astype(BF16)


def _cast_rows(x, transpose):
    r, d = x.shape
    tr = _pick(r, 512, 256, 128)
    if transpose:
        kern, oshape, ospec = _cast_t_kernel, (d, r), pl.BlockSpec((d, tr), lambda i: (0, i))
    else:
        kern, oshape, ospec = _cast_kernel, (r, d), pl.BlockSpec((tr, d), lambda i: (i, 0))
    return pl.pallas_call(
        kern,
        out_shape=jax.ShapeDtypeStruct(oshape, BF16),
        grid=(r // tr,),
        in_specs=[pl.BlockSpec((tr, d), lambda i: (i, 0))],
        out_specs=ospec,
        compiler_params=_cparams("parallel"),
        name="cast_t" if transpose else "cast",
    )(x)


def _final_kernel(xf_ref, xb_ref, ch_ref, p_ref, wg_ref, wp_ref, g_ref, b_ref, o_ref, acc0_ref, acc1_ref, *, tn,
                  nchunks):
    j = pl.program_id(1)

    def step(fill_ref, done_ref):
        cols = pl.ds(pl.multiple_of(j * tn, tn), tn)
        gate = _sigmoid(_dot(xb_ref[...], wg_ref[...]))
        proj = _dot(p_ref[...].astype(BF16), wp_ref[...])
        fill_ref[:, cols] = DEEPNORM_ALPHA * xf_ref[...] + ch_ref[...] + gate * proj

        def norm_chunk():
            rows = pl.ds(pl.multiple_of(j * LANES, LANES), LANES)
            o_ref[...] = _layer_norm(done_ref[rows, :], g_ref[...], b_ref[...])

        _lag_epilogue(j, nchunks, pl.num_programs(1), norm_chunk)

    _lag_run(acc0_ref, acc1_ref, step)


def _final(x1f, x1b, ch, p, wg_b, wp_b, g, b):
    n, d = x1f.shape
    pd = p.shape[1]
    tn = 1024
    ns = d // tn
    tm, mt, nchunks, cur, chunk = _lag_specs(n, ns)
    vec = pl.BlockSpec((1, d), lambda i, j: (0, 0))
    rows = lambda w: pl.BlockSpec((tm, w), lambda i, j: (cur(i), 0), pipeline_mode=pl.Buffered(1))
    tile = pl.BlockSpec((tm, tn), lambda i, j: (cur(i), j))
    return pl.pallas_call(
        functools.partial(_final_kernel, tn=tn, nchunks=nchunks),
        out_shape=jax.ShapeDtypeStruct((n, d), F32),
        grid=(mt + 1, ns),
        in_specs=[tile, rows(d), tile, rows(pd),
                  pl.BlockSpec((d, tn), lambda i, j: (0, j)),
                  pl.BlockSpec((pd, tn), lambda i, j: (0, j)),
                  vec, vec],
        out_specs=pl.BlockSpec((LANES, d), lambda i, j: (chunk(i, j), 0)),
        scratch_shapes=[pltpu.VMEM((tm, d), F32), pltpu.VMEM((tm, d), F32)],
        compiler_params=_cparams("arbitrary", "arbitrary"),
        name="ple_ln2",
    )(x1f, x1b, ch, p, wg_b, wp_b, g.reshape(1, d), b.reshape(1, d))


def _rope_tables(pos, half):
    inv = ROPE_BASE ** (-jnp.arange(half, dtype=F32) / half)
    ang = pos.astype(F32)[:, None] * inv[None]
    return jnp.cos(ang), jnp.sin(ang)


def _post_mixer(o_r, o_m, x_raw, p, wts):
    x1f, x1b, x1t = _out_proj_ln1(o_r, o_m, wts["w_out"], x_raw, wts["ln_emb_g"], wts["ln_emb_b"], wts["ln1_g"],
                                  wts["ln1_b"])
    heads = wts["keys"].shape[0] // 2
    scores = _peer_scores(x1b, wts["w_q"], wts["keys"])
    e1, rb, e2, b2 = _peer_route(scores, heads)
    ch = _peer_dense(x1t, wts["u"], wts["vt"], e1, rb, e2, b2)
    return _final(x1f, x1b, ch, p, wts["w_gate"], wts["w_proj"], wts["ln2_g"], wts["ln2_b"])


def kernel(x_prompt, x_sample, state_ret, state_conv, state_mlstm_c, state_mlstm_n, state_mlstm_m, p_prompt, p_sample,
           ln_emb_g, ln_emb_b, w_in, b_gate, conv_w, conv_b, g_ret_norm, g_ml_norm, w_out, ln1_g, ln1_b,
           w_peer_q, peer_sub_keys, peer_u, peer_v, w_ple_gate, w_ple_proj, ln2_g, ln2_b):
    bsz, t, d = x_prompt.shape
    nb = x_sample.shape[0]
    _, _, heads, dk, _ = state_ret.shape
    rw = heads * dk
    n_main = 8 * rw
    assert x_sample.shape[1] == 1 and w_in.shape[0] == DEPTH and t % CHUNK == 0

    w_in_t = jnp.swapaxes(w_in, 1, 2)
    wg_t = w_in_t[0, n_main:, :].astype(BF16)
    log_g = jnp.log1p(-(2.0 ** (-5.0 - jnp.arange(heads, dtype=F32))))
    keys = peer_sub_keys[0]
    wts = dict(
        ln_emb_g=ln_emb_g, ln_emb_b=ln_emb_b, ln1_g=ln1_g[0], ln1_b=ln1_b[0], ln2_g=ln2_g[0], ln2_b=ln2_b[0],
        w_out=w_out[0].astype(BF16), w_q=w_peer_q[0].astype(BF16),
        keys=keys.reshape(keys.shape[0] * 2, keys.shape[2], keys.shape[3]).astype(BF16),
        u=_cast_rows(peer_u[0], False), vt=_cast_rows(peer_v[0], True),
        w_gate=w_ple_gate[0].astype(BF16), w_proj=w_ple_proj[0].astype(BF16),
    )

    xp = x_prompt.reshape(bsz * t, d)
    xn = _ln_cast(xp, ln_emb_g, ln_emb_b)
    z = _in_proj(xn, w_in_t, n_main)
    gates_t = _gate_rows(xn, wg_t)
    cos, sin = _rope_tables(jnp.arange(t), dk // 2)
    o_r, ret_p = _ret_prompt(z, log_g, cos, sin, g_ret_norm[0], bsz, t, heads, dk, 0)
    o_m, c_p, n_p, m_p = _mlstm_prompt(z, gates_t, b_gate[0], conv_w[0], conv_b[0].reshape(1, 2 * rw), g_ml_norm[0],
                                       bsz, t, heads, dk, 4 * rw)
    conv_p = z.reshape(bsz, t, n_main)[:, t - (CONV_W - 1):, 4 * rw:6 * rw]
    y_prompt = _post_mixer(o_r, o_m, xp, p_prompt[0].reshape(bsz * t, -1), wts).reshape(bsz, t, d)

    xs = x_sample.reshape(nb, d)
    xns = _ln_cast(xs, ln_emb_g, ln_emb_b)
    zs = _in_proj(xns, w_in_t, n_main)
    gates_s = _gate_rows(xns, wg_t).T
    cs, sn = _rope_tables(jnp.full((1,), PAST_LEN), dk // 2)
    o_rs, o_ms, ret_s, conv_s, c_s, n_s, m_s = _mix_sample(
        zs, gates_s, log_g, b_gate[0], jnp.concatenate([cs, sn], axis=0), state_ret[0], state_conv[0],
        state_mlstm_c[0], state_mlstm_n[0], state_mlstm_m[0], conv_w[0], conv_b[0], g_ret_norm[0], g_ml_norm[0], heads, dk)
    y_sample = _post_mixer(o_rs, o_ms, xs, p_sample[0].reshape(nb, -1), wts).reshape(nb, 1, d)

    lead = lambda a: a[None]
    return (y_prompt, y_sample, lead(ret_p), lead(conv_p), lead(c_p), lead(n_p), lead(m_p),
            lead(ret_s), lead(conv_s), lead(c_s), lead(n_s), lead(m_s))
```

```python
import functools
import math

import jax
import jax.numpy as jnp
from jax import lax
from jax.experimental import pallas as pl
from jax.experimental.pallas import tpu as pltpu

F32 = jnp.float32
BF16 = jnp.bfloat16
I32 = jnp.int32

LN_EPS = 1e-5
DEPTH = 1
DEEPNORM_ALPHA = (2.0 * DEPTH) ** 0.25
CHUNK = 128
ROPE_BASE = 10000.0
PAST_LEN = 16384
PEER_TOPK = 16
PEER_TE = 512
CONV_W = 4

V7X_VMEM_LIMIT_BYTES = 56 * 1024 * 1024
LANES = 128


def _cparams(*sem):
    return pltpu.CompilerParams(dimension_semantics=sem, vmem_limit_bytes=V7X_VMEM_LIMIT_BYTES)


def _dot(a, b):
    return jnp.dot(a, b, preferred_element_type=F32)


def _dot_nt(a, b):
    return lax.dot_general(a, b, (((1,), (1,)), ((), ())), preferred_element_type=F32)


def _dot_tn(a, b):
    return lax.dot_general(a, b, (((0,), (0,)), ((), ())), preferred_element_type=F32)


def _sigmoid(x):
    return 1.0 / (1.0 + jnp.exp(-x))


def _layer_norm(x, g, b):
    mu = jnp.mean(x, axis=-1, keepdims=True)
    xc = x - mu
    var = jnp.mean(xc * xc, axis=-1, keepdims=True)
    return xc * lax.rsqrt(var + LN_EPS) * g + b


def _head_norm(x, g):
    mu = jnp.mean(x, axis=-1, keepdims=True)
    xc = x - mu
    var = jnp.mean(xc * xc, axis=-1, keepdims=True)
    return xc * lax.rsqrt(var + LN_EPS) * g


def _pick(n, *cands):
    for c in cands:
        if n % c == 0:
            return c
    return n


def _ln_cast_kernel(x_ref, g_ref, b_ref, o_ref):
    o_ref[...] = _layer_norm(x_ref[...], g_ref[...], b_ref[...]).astype(BF16)


def _ln_cast(x, g, b):
    n, d = x.shape
    tm = _pick(n, 256, 128)
    return pl.pallas_call(
        _ln_cast_kernel,
        out_shape=jax.ShapeDtypeStruct((n, d), BF16),
        grid=(n // tm,),
        in_specs=[pl.BlockSpec((tm, d), lambda i: (i, 0)),
                  pl.BlockSpec((1, d), lambda i: (0, 0)),
                  pl.BlockSpec((1, d), lambda i: (0, 0))],
        out_specs=pl.BlockSpec((tm, d), lambda i: (i, 0)),
        compiler_params=_cparams("parallel"),
        name="ln_cast",
    )(x, g.reshape(1, d), b.reshape(1, d))


def _in_proj_kernel(x_ref, w_ref, o_ref, wb_ref):
    @pl.when(pl.program_id(1) == 0)
    def _():
        wb_ref[...] = w_ref[...].astype(BF16)

    o_ref[...] = _dot_nt(x_ref[...], wb_ref[...])


def _in_proj(xn, w_in_t, n_main):
    n, d = xn.shape
    tm = _pick(n, 1024, 512, 256, 128)
    tn = 512
    return pl.pallas_call(
        _in_proj_kernel,
        out_shape=jax.ShapeDtypeStruct((n, n_main), F32),
        grid=(n_main // tn, n // tm),
        in_specs=[pl.BlockSpec((tm, d), lambda j, i: (i, 0)),
                  pl.BlockSpec((None, tn, d), lambda j, i: (0, j, 0))],
        out_specs=pl.BlockSpec((tm, tn), lambda j, i: (i, j)),
        scratch_shapes=[pltpu.VMEM((tn, d), BF16)],
        compiler_params=_cparams("parallel", "arbitrary"),
        name="in_proj",
    )(xn, w_in_t)


def _gate_rows_kernel(w_ref, x_ref, o_ref):
    o_ref[...] = _dot_nt(w_ref[...], x_ref[...])


def _gate_rows(xn, wg_t):
    n, d = xn.shape
    g = wg_t.shape[0]
    tb = _pick(n, 1024, 512, 256, 128)
    return pl.pallas_call(
        _gate_rows_kernel,
        out_shape=jax.ShapeDtypeStruct((g, n), F32),
        grid=(n // tb,),
        in_specs=[pl.BlockSpec((g, d), lambda i: (0, 0)),
                  pl.BlockSpec((tb, d), lambda i: (i, 0))],
        out_specs=pl.BlockSpec((g, tb), lambda i: (0, i)),
        compiler_params=_cparams("parallel"),
        name="gate_rows",
    )(wg_t, xn)


def _rope(x, cos, sin):
    half = x.shape[-1] // 2
    x1, x2 = x[:, :half], x[:, half:]
    return jnp.concatenate([x1 * cos - x2 * sin, x1 * sin + x2 * cos], axis=-1)


def _log_sigmoid(x):
    return jnp.minimum(x, 0.0) - jnp.log1p(jnp.exp(-jnp.abs(x)))


def _row_to_col(row, eye):
    return jnp.sum(jnp.where(eye, row, 0.0), axis=1, keepdims=True)


HEAD_GROUP = 2


def _ret_prompt_kernel(lg_ref, q_ref, k_ref, v_ref, g_ref, cos_ref, sin_ref, gn_ref, o_ref, s_ref, *, chunk, dk):
    grp = pl.program_id(1)
    L = chunk
    t = q_ref.shape[0]
    hg = q_ref.shape[1] // dk
    ii = lax.broadcasted_iota(I32, (L, L), 0)
    jj = lax.broadcasted_iota(I32, (L, L), 1)
    causal = ii >= jj
    diff = jnp.where(causal, (ii - jj).astype(F32), 0.0)
    idx = lax.broadcasted_iota(I32, (L, 1), 0).astype(F32)
    scale = dk ** -0.5
    s_ref[...] = jnp.zeros_like(s_ref)
    per_head = []
    for u in range(hg):
        lg = lg_ref[grp * hg + u]
        per_head.append(dict(
            decay_in=jnp.where(causal, jnp.exp(lg * diff), 0.0),
            decay_q=jnp.exp(lg * (idx + 1.0)),
            decay_k=jnp.exp(lg * (float(L) - 1.0 - idx)),
            decay_c=jnp.exp(lg * jnp.full((1, 1), float(L), F32)),
            gn=gn_ref[pl.ds(grp * hg + u, 1), :]))

    def body(c, carry):
        rows = pl.ds(pl.multiple_of(c * L, L), L)
        cos, sin = cos_ref[rows, :], sin_ref[rows, :]
        for u, hd in enumerate(per_head):
            cols = slice(u * dk, (u + 1) * dk)
            rq = _rope(q_ref[rows, cols], cos, sin)
            rk = _rope(k_ref[rows, cols], cos, sin) * scale
            vb = v_ref[rows, cols].astype(BF16)
            rqb = rq.astype(BF16)
            s = s_ref[u]
            sc = _dot_nt(rqb, rk.astype(BF16)) * hd["decay_in"]
            o = _dot(sc.astype(BF16), vb) + _dot(rqb, s.astype(BF16)) * hd["decay_q"]
            s_ref[u] = s * hd["decay_c"] + _dot_tn((rk * hd["decay_k"]).astype(BF16), vb)
            g = g_ref[rows, cols]
            o_ref[rows, cols] = (_head_norm(o, hd["gn"]) * (g * _sigmoid(g))).astype(o_ref.dtype)
        return carry

    lax.fori_loop(0, t // L, body, 0)


def _ret_prompt(z, log_g, cos, sin, g_norm, bsz, t, heads, dk, col0):
    hg = HEAD_GROUP if heads % HEAD_GROUP == 0 else 1
    ng = heads // hg
    gb = col0 // (hg * dk)
    zspec = lambda off: pl.BlockSpec((t, hg * dk), lambda b, g, *_: (b, gb + off * ng + g))
    grid_spec = pltpu.PrefetchScalarGridSpec(
        num_scalar_prefetch=1,
        grid=(bsz, ng),
        in_specs=[zspec(0), zspec(1), zspec(2), zspec(3),
                  pl.BlockSpec((t, dk // 2), lambda b, g, *_: (0, 0)),
                  pl.BlockSpec((t, dk // 2), lambda b, g, *_: (0, 0)),
                  pl.BlockSpec((heads, dk), lambda b, g, *_: (0, 0))],
        out_specs=[pl.BlockSpec((t, hg * dk), lambda b, g, *_: (b, g)),
                   pl.BlockSpec((None, hg, dk, dk), lambda b, g, *_: (b, g, 0, 0))],
    )
    return pl.pallas_call(
        functools.partial(_ret_prompt_kernel, chunk=CHUNK, dk=dk),
        out_shape=[jax.ShapeDtypeStruct((bsz * t, heads * dk), BF16),
                   jax.ShapeDtypeStruct((bsz, heads, dk, dk), F32)],
        grid_spec=grid_spec,
        compiler_params=_cparams("parallel", "parallel"),
        name="ret_prompt",
    )(log_g, z, z, z, z, cos, sin, g_norm)


def _mlstm_prompt_kernel(bg_ref, xq_ref, xk_ref, v_ref, og_ref, ig_ref, fg_ref, cwq_ref, cwk_ref, cbq_ref, cbk_ref,
                         gn_ref, o_ref, c_ref, n_ref, m_ref, bt_s, ig_s, *, chunk, heads, dk):
    grp = pl.program_id(1)
    L = chunk
    t = xq_ref.shape[0]
    hg = xq_ref.shape[1] // dk
    nc = t // L
    scale = dk ** -0.5

    lane = lax.broadcasted_iota(I32, (nc, L), 1)
    for u in range(hg):
        h = grp * hg + u
        ig_s[u] = ig_ref[u] + bg_ref[h]
        bt = _log_sigmoid(fg_ref[u] + bg_ref[heads + h])
        s = 1
        while s < L:
            bt = bt + jnp.where(lane >= s, pltpu.roll(bt, s, axis=1), 0.0)
            s *= 2
        bt_s[u] = bt

    ii = lax.broadcasted_iota(I32, (L, L), 0)
    jj = lax.broadcasted_iota(I32, (L, L), 1)
    causal = ii >= jj
    eye = ii == jj
    row = lax.broadcasted_iota(I32, (L, 1), 0)
    gns = [gn_ref[pl.ds(grp * hg + u, 1), :] for u in range(hg)]
    c_ref[...] = jnp.zeros_like(c_ref)
    n_ref[...] = jnp.zeros_like(n_ref)

    def conv_silu(x_ref, w_ref, b_ref, c, rows, cols):
        x = x_ref[rows, cols]
        prev_rows = pl.ds(pl.multiple_of(jnp.maximum(c - 1, 0) * L, L), L)
        xp = jnp.where(c > 0, x_ref[prev_rows, cols], 0.0)
        y = x * w_ref[CONV_W - 1:CONV_W, cols] + b_ref[:, cols]
        for j in range(1, CONV_W):
            xs = jnp.where(row < j, pltpu.roll(xp, j, axis=0), pltpu.roll(x, j, axis=0))
            y = y + xs * w_ref[CONV_W - 1 - j:CONV_W - j, cols]
        return y * _sigmoid(y)

    def head_step(u, c, rows, m):
        cols = slice(u * dk, (u + 1) * dk)
        q = conv_silu(xq_ref, cwq_ref, cbq_ref, c, rows, cols)
        k = conv_silu(xk_ref, cwk_ref, cbk_ref, c, rows, cols) * scale
        vb = v_ref[rows, cols].astype(BF16)
        qb = q.astype(BF16)
        bt_row = bt_s[u, pl.ds(c, 1), :]
        ig_row = ig_s[u, pl.ds(c, 1), :]
        bt_col = _row_to_col(bt_row, eye)
        ig_col = _row_to_col(ig_row, eye)
        dmat = jnp.where(causal, bt_col - bt_row + ig_row, -jnp.inf)
        prior = bt_col + m
        mt = jnp.maximum(prior, jnp.max(dmat, axis=1, keepdims=True))
        w = jnp.exp(dmat - mt)
        wp = jnp.exp(prior - mt)
        qk = _dot_nt(qb, k.astype(BF16)) * w
        cst = c_ref[u]
        nst = n_ref[u]
        num = _dot(qk.astype(BF16), vb) + _dot(qb, cst.astype(BF16)) * wp
        den = jnp.sum(qk, axis=1, keepdims=True) + jnp.sum(q * nst, axis=1, keepdims=True) * wp
        hh = num / jnp.maximum(jnp.abs(den), jnp.exp(-mt))
        bl = bt_row[:, L - 1:L]
        m_new = mt[L - 1:L, :]
        wk = jnp.exp(bl - bt_col + ig_col - m_new)
        wc = jnp.exp(bl + m - m_new)
        kw = k * wk
        c_ref[u] = cst * wc + _dot_tn(kw.astype(BF16), vb)
        n_ref[u] = nst * wc + jnp.sum(kw, axis=0, keepdims=True)
        og = og_ref[rows, cols]
        o_ref[rows, cols] = (_head_norm(hh, gns[u]) * _sigmoid(og)).astype(o_ref.dtype)
        return m_new

    def body(c, ms):
        rows = pl.ds(pl.multiple_of(c * L, L), L)
        return tuple(head_step(u, c, rows, ms[u]) for u in range(hg))

    m_fin = lax.fori_loop(0, nc, body, tuple(jnp.zeros((1, 1), F32) for _ in range(hg)))
    for u in range(hg):
        m_ref[u] = jnp.broadcast_to(m_fin[u], m_ref.shape[1:])


def _mlstm_prompt(z, gates_t, b_gate, conv_w, conv_b, g_norm, bsz, t, heads, dk, col0):
    hg = HEAD_GROUP if heads % HEAD_GROUP == 0 else 1
    ng = heads // hg
    gb = col0 // (hg * dk)
    nc = t // CHUNK
    zspec = lambda off: pl.BlockSpec((t, hg * dk), lambda b, g, *_: (b, gb + off * ng + g))
    gates4 = gates_t.reshape(2 * heads, bsz, nc, CHUNK)
    gspec = lambda off: pl.BlockSpec((hg, None, nc, CHUNK), lambda b, g, *_: (off * ng + g, b, 0, 0))
    wspec = lambda rows, off: pl.BlockSpec((rows, hg * dk), lambda b, g, *_: (0, off * ng + g))
    grid_spec = pltpu.PrefetchScalarGridSpec(
        num_scalar_prefetch=1,
        grid=(bsz, ng),
        in_specs=[zspec(0), zspec(1), zspec(2), zspec(3), gspec(0), gspec(1),
                  wspec(CONV_W, 0), wspec(CONV_W, 1), wspec(1, 0), wspec(1, 1),
                  pl.BlockSpec((heads, dk), lambda b, g, *_: (0, 0))],
        out_specs=[pl.BlockSpec((t, hg * dk), lambda b, g, *_: (b, g)),
                   pl.BlockSpec((None, hg, dk, dk), lambda b, g, *_: (b, g, 0, 0)),
                   pl.BlockSpec((None, hg, 1, dk), lambda b, g, *_: (b, g, 0, 0)),
                   pl.BlockSpec((None, hg, 1, LANES), lambda b, g, *_: (b, g, 0, 0))],
        scratch_shapes=[pltpu.VMEM((hg, nc, CHUNK), F32), pltpu.VMEM((hg, nc, CHUNK), F32)],
    )
    o, c, n, m = pl.pallas_call(
        functools.partial(_mlstm_prompt_kernel, chunk=CHUNK, heads=heads, dk=dk),
        out_shape=[jax.ShapeDtypeStruct((bsz * t, heads * dk), BF16),
                   jax.ShapeDtypeStruct((bsz, heads, dk, dk), F32),
                   jax.ShapeDtypeStruct((bsz, heads, 1, dk), F32),
                   jax.ShapeDtypeStruct((bsz, heads, 1, LANES), F32)],
        grid_spec=grid_spec,
        compiler_params=_cparams("parallel", "parallel"),
        name="mlstm_prompt",
    )(b_gate, z, z, z, z, gates4, gates4, conv_w, conv_w, conv_b, conv_b, g_norm)
    return o, c, n[:, :, 0, :], m[:, :, 0, 0]


def _pad_rows(row, rows=8):
    r = lax.broadcasted_iota(I32, (rows, row.shape[1]), 0)
    return jnp.where(r == 0, row, 0.0)


def _mix_sample_kernel(lg_ref, bg_ref, z_ref, gt_ref, cs_ref, sr_ref, cb_ref, cc_ref, cn_ref, cm_ref, cw_ref, cbias_ref,
                       gr_ref, gm_ref, or_ref, om_ref, sro_ref, cbo_ref, cco_ref, cno_ref, cmo_ref, *, heads, dk):
    rw = heads * dk
    cos, sin = cs_ref[0:1, :], cs_ref[1:2, :]
    scale = dk ** -0.5
    xqk = z_ref[:, 4 * rw:6 * rw]
    buf = cb_ref[...]
    y = xqk * cw_ref[CONV_W - 1:CONV_W, :] + cbias_ref[...]
    for j in range(CONV_W - 1):
        y = y + buf[j:j + 1, :] * cw_ref[j:j + 1, :]
    qk_act = y * _sigmoid(y)
    cbo_ref[0:CONV_W - 2, :] = buf[1:CONV_W - 1, :]
    cbo_ref[CONV_W - 2:CONV_W - 1, :] = xqk

    for h in range(heads):
        sl = lambda g: slice(g * rw + h * dk, g * rw + (h + 1) * dk)
        gam = jnp.exp(jnp.full((1, 1), lg_ref[h], F32))
        rq = _rope(z_ref[:, sl(0)], cos, sin)
        rk = _rope(z_ref[:, sl(1)], cos, sin) * scale
        v = z_ref[:, sl(2)]
        rg = z_ref[:, sl(3)]
        s = sr_ref[h]
        sc = jnp.sum(rq * rk, axis=1, keepdims=True)
        qs = _dot(jnp.broadcast_to(rq, (8, dk)).astype(BF16), s.astype(BF16))[0:1, :]
        o = sc * v + qs * gam
        sro_ref[h] = s * gam + _dot_tn(_pad_rows(rk).astype(BF16), jnp.broadcast_to(v, (8, dk)).astype(BF16))
        or_ref[:, h * dk:(h + 1) * dk] = (_head_norm(o, gr_ref[h:h + 1, :]) * (rg * _sigmoid(rg))).astype(or_ref.dtype)
        q = qk_act[:, h * dk:(h + 1) * dk]
        k = qk_act[:, rw + h * dk:rw + (h + 1) * dk] * scale
        v = z_ref[:, sl(6)]
        og = z_ref[:, sl(7)]
        it = gt_ref[:, h:h + 1] + bg_ref[h]
        lf = _log_sigmoid(gt_ref[:, heads + h:heads + h + 1] + bg_ref[heads + h])
        m = cm_ref[:, h:h + 1]
        cst = cc_ref[h]
        nst = cn_ref[h:h + 1, :]
        prior = lf + m
        mt = jnp.maximum(prior, it)
        w = jnp.exp(it - mt)
        wp = jnp.exp(prior - mt)
        qk = jnp.sum(q * k, axis=1, keepdims=True) * w
        qc = _dot(jnp.broadcast_to(q, (8, dk)).astype(BF16), cst.astype(BF16))[0:1, :]
        num = qk * v + qc * wp
        den = qk + jnp.sum(q * nst, axis=1, keepdims=True) * wp
        hh = num / jnp.maximum(jnp.abs(den), jnp.exp(-mt))
        wk = jnp.exp(it - mt)
        wc = jnp.exp(lf + m - mt)
        kw = k * wk
        cco_ref[h] = cst * wc + _dot_tn(_pad_rows(kw).astype(BF16), jnp.broadcast_to(v, (8, dk)).astype(BF16))
        cno_ref[h:h + 1, :] = nst * wc + kw
        cmo_ref[:, h:h + 1] = mt
        om_ref[:, h * dk:(h + 1) * dk] = (_head_norm(hh, gm_ref[h:h + 1, :]) * _sigmoid(og)).astype(om_ref.dtype)


def _mix_sample(z, gates, log_g, b_gate, cos_sin, s_ret, s_conv, s_c, s_n, s_m, conv_w, conv_b, g_ret, g_ml, heads, dk):
    nb = z.shape[0]
    rw = heads * dk
    per_b3 = lambda *tail: pl.BlockSpec((None,) + tail, lambda b, *_: (b,) + (0,) * len(tail))
    whole = lambda a: pl.BlockSpec(a.shape, lambda b, *_: (0,) * a.ndim)
    z3 = z.reshape(nb, 1, 8 * rw)
    g3 = gates.reshape(nb, 1, 2 * heads)
    m3 = s_m.reshape(nb, 1, heads)
    cb2 = conv_b.reshape(1, 2 * rw)
    grid_spec = pltpu.PrefetchScalarGridSpec(
        num_scalar_prefetch=2,
        grid=(nb,),
        in_specs=[per_b3(1, 8 * rw), per_b3(1, 2 * heads), whole(cos_sin),
                  per_b3(heads, dk, dk), per_b3(CONV_W - 1, 2 * rw), per_b3(heads, dk, dk), per_b3(heads, dk),
                  per_b3(1, heads), whole(conv_w), whole(cb2), whole(g_ret), whole(g_ml)],
        out_specs=[per_b3(1, rw), per_b3(1, rw), per_b3(heads, dk, dk), per_b3(CONV_W - 1, 2 * rw),
                   per_b3(heads, dk, dk), per_b3(heads, dk), per_b3(1, heads)],
    )
    o_r, o_m, sr, cb, cc, cn, cm = pl.pallas_call(
        functools.partial(_mix_sample_kernel, heads=heads, dk=dk),
        out_shape=[jax.ShapeDtypeStruct((nb, 1, rw), BF16), jax.ShapeDtypeStruct((nb, 1, rw), BF16),
                   jax.ShapeDtypeStruct(s_ret.shape, F32), jax.ShapeDtypeStruct(s_conv.shape, F32),
                   jax.ShapeDtypeStruct(s_c.shape, F32), jax.ShapeDtypeStruct(s_n.shape, F32),
                   jax.ShapeDtypeStruct((nb, 1, heads), F32)],
        grid_spec=grid_spec,
        compiler_params=_cparams("parallel"),
        name="mix_sample",
    )(log_g, b_gate, z3, g3, cos_sin, s_ret, s_conv, s_c, s_n, m3, conv_w, cb2, g_ret, g_ml)
    return o_r.reshape(nb, rw), o_m.reshape(nb, rw), sr, cb, cc, cn, cm.reshape(nb, heads)


def _lag_specs(n, ns):
    tm = _pick(n, *(c for c in (512, 256, 128) if c // LANES <= ns))
    mt = n // tm
    nchunks = tm // LANES
    cur = lambda i: jnp.minimum(i, mt - 1)
    chunk = lambda i, j: jnp.maximum(i - 1, 0) * nchunks + jnp.where(i == 0, 0, jnp.minimum(j, nchunks - 1))
    return tm, mt, nchunks, cur, chunk


def _lag_epilogue(j, nchunks, ns, fn):
    if nchunks == ns:
        fn()
    else:
        pl.when(j < nchunks)(fn)


def _lag_run(acc0_ref, acc1_ref, step):
    i, j = pl.program_id(0), pl.program_id(1)

    @pl.when((i == 0) & (j == 0))
    def _():
        acc1_ref[...] = jnp.zeros_like(acc1_ref)

    @pl.when(i % 2 == 0)
    def _():
        step(acc0_ref, acc1_ref)

    @pl.when(i % 2 == 1)
    def _():
        step(acc1_ref, acc0_ref)


def _out_proj_kernel(a_ref, b_ref, wa_ref, wb_ref, x_ref, eg_ref, eb_ref, g_ref, bb_ref, of_ref, ob_ref, ot_ref,
                     acc0_ref, acc1_ref, *, tn, nchunks):
    j = pl.program_id(1)

    def step(fill_ref, done_ref):
        cols = pl.ds(pl.multiple_of(j * tn, tn), tn)
        fill_ref[:, cols] = _dot(a_ref[...], wa_ref[...]) + _dot(b_ref[...], wb_ref[...])

        def norm_chunk():
            rows = pl.ds(pl.multiple_of(j * LANES, LANES), LANES)
            xe = _layer_norm(x_ref[...], eg_ref[...], eb_ref[...])
            x1 = _layer_norm(DEEPNORM_ALPHA * xe + done_ref[rows, :], g_ref[...], bb_ref[...])
            of_ref[...] = x1
            ob_ref[...] = x1.astype(BF16)
            ot_ref[...] = x1.T.astype(BF16)

        _lag_epilogue(j, nchunks, pl.num_programs(1), norm_chunk)

    _lag_run(acc0_ref, acc1_ref, step)


def _out_proj_ln1(o_r, o_m, w_out_b, x_raw, eg, eb, g, b):
    n, d = x_raw.shape
    ka = o_r.shape[1]
    tn = 1024
    ns = d // tn
    tm, mt, nchunks, cur, chunk = _lag_specs(n, ns)
    row = lambda a: a.reshape(1, d)
    vec = pl.BlockSpec((1, d), lambda i, j: (0, 0))
    return pl.pallas_call(
        functools.partial(_out_proj_kernel, tn=tn, nchunks=nchunks),
        out_shape=[jax.ShapeDtypeStruct((n, d), F32), jax.ShapeDtypeStruct((n, d), BF16),
                   jax.ShapeDtypeStruct((d, n), BF16)],
        grid=(mt + 1, ns),
        in_specs=[pl.BlockSpec((tm, ka), lambda i, j: (cur(i), 0), pipeline_mode=pl.Buffered(1)),
                  pl.BlockSpec((tm, ka), lambda i, j: (cur(i), 0), pipeline_mode=pl.Buffered(1)),
                  pl.BlockSpec((ka, tn), lambda i, j: (0, j)),
                  pl.BlockSpec((ka, tn), lambda i, j: (1, j)),
                  pl.BlockSpec((LANES, d), lambda i, j: (chunk(i, j), 0)),
                  vec, vec, vec, vec],
        out_specs=[pl.BlockSpec((LANES, d), lambda i, j: (chunk(i, j), 0)),
                   pl.BlockSpec((LANES, d), lambda i, j: (chunk(i, j), 0)),
                   pl.BlockSpec((d, LANES), lambda i, j: (0, chunk(i, j)))],
        scratch_shapes=[pltpu.VMEM((tm, d), F32), pltpu.VMEM((tm, d), F32)],
        compiler_params=_cparams("arbitrary", "arbitrary"),
        name="out_proj_ln1",
    )(o_r, o_m, w_out_b, w_out_b, x_raw, row(eg), row(eb), row(g), row(b))


def _peer_scores_kernel(x_ref, wq_ref, keys_ref, o_ref):
    q = _dot(x_ref[...], wq_ref[...])
    nk, kd = keys_ref.shape[1], keys_ref.shape[2]
    for hp in range(keys_ref.shape[0]):
        qh = q[:, hp * kd:(hp + 1) * kd].astype(BF16)
        o_ref[hp] = _dot_nt(keys_ref[hp], qh)


def _peer_scores(x1b, wq_b, keys_b):
    n, d = x1b.shape
    hp, nk, kd = keys_b.shape
    tb = _pick(n, 512, 256, 128)
    return pl.pallas_call(
        _peer_scores_kernel,
        out_shape=jax.ShapeDtypeStruct((hp, nk, n), F32),
        grid=(n // tb,),
        in_specs=[pl.BlockSpec((tb, d), lambda i: (i, 0)),
                  pl.BlockSpec(wq_b.shape, lambda i: (0, 0), pipeline_mode=pl.Buffered(1)),
                  pl.BlockSpec(keys_b.shape, lambda i: (0, 0, 0))],
        out_specs=pl.BlockSpec((hp, nk, tb), lambda i: (0, 0, i)),
        compiler_params=_cparams("parallel"),
        name="peer_scores",
    )(x1b, wq_b, keys_b)


def _top_ranks(s, k):
    n, tb = s.shape
    idx = lax.broadcasted_iota(I32, (n, tb), 0)
    rank = jnp.full((n, tb), k, I32)
    vals = []
    for r in range(k):
        m = jnp.max(s, axis=0, keepdims=True)
        first = jnp.min(jnp.where(s == m, idx, n), axis=0, keepdims=True)
        hit = idx == first
        rank = jnp.where(hit, r, rank)
        s = jnp.where(hit, -jnp.inf, s)
        vals.append(m)
    return jnp.concatenate(vals, axis=0), rank


def _route_head_exact(s1, s2, topk):
    a, r1 = _top_ranks(s1, topk)
    b, r2 = _top_ranks(s2, topk)
    cand = jnp.concatenate([a[r:r + 1, :] + b for r in range(topk)], axis=0)
    tv, rc = _top_ranks(cand, topk)
    z = jnp.sum(jnp.exp(tv - tv[0:1, :]), axis=0, keepdims=True)
    c1 = jnp.zeros(s1.shape, I32)
    for r in range(topk):
        sel = rc[r * topk:(r + 1) * topk, :] < topk
        c1 = jnp.where(r1 == r, jnp.sum(sel.astype(I32), axis=0, keepdims=True), c1)
    return jnp.exp(s1 - a[0:1, :]) / z, c1, jnp.exp(s2 - b[0:1, :]), r2


def _top_values(s, k, on_hit):
    vals = []
    for r in range(k):
        m = jnp.max(s, axis=0, keepdims=True)
        hit = s == m
        on_hit(r, hit)
        s = jnp.where(hit, -jnp.inf, s)
        vals.append(m)
    return vals


def _route_head_fast(s1, s2, topk):
    n, tb = s1.shape
    sub = 8
    assert topk == 2 * sub
    st = dict(r1=jnp.full((n, tb), topk, I32), r2=jnp.full((n, tb), topk, I32))

    def hit1(r, hit):
        st["r1"] = jnp.where(hit, r, st["r1"])

    def hit2(r, hit):
        st["r2"] = jnp.where(hit, r, st["r2"])

    a = _top_values(s1, topk, hit1)
    b = _top_values(s2, topk, hit2)
    r1, r2 = st["r1"], st["r2"]
    a_arr = jnp.concatenate(a, axis=0)
    b_arr = jnp.concatenate(b, axis=0)
    row = lax.broadcasted_iota(I32, (sub, tb), 0)
    slabs, cols = [], []
    for q2 in range(sub):
        lim = topk // (q2 + 1)
        for r1s in range(0, lim, sub):
            slab = a_arr[r1s:r1s + sub, :] + b_arr[q2:q2 + 1, :]
            if lim - r1s < sub:
                slab = jnp.where(row < lim - r1s, slab, -jnp.inf)
            slabs.append(slab)
            cols.append((r1s, q2))
    slabs.append(a_arr[0:1, :] + b_arr[sub:topk, :])
    cand = jnp.concatenate(slabs, axis=0)
    st["sel"] = jnp.zeros(cand.shape, I32)

    def hit3(r, hit):
        st["sel"] = jnp.where(hit, 1, st["sel"])

    tv = _top_values(cand, topk, hit3)
    sel = st["sel"]
    count = lambda m: jnp.sum(m.astype(I32), axis=0, keepdims=True)
    rowcount = [jnp.zeros((sub, tb), I32) for _ in range(topk // sub)]
    for i, (r1s, _) in enumerate(cols):
        rowcount[r1s // sub] = rowcount[r1s // sub] + sel[i * sub:(i + 1) * sub, :]
    rowcount[0] = rowcount[0] + jnp.where(row == 0, count(sel[len(cols) * sub:, :]), 0)
    ok = (count(r1 < topk) == topk) & (count(r2 < topk) == topk) & (count(sel) == topk)
    z = jnp.ones_like(tv[0])
    for r in range(1, topk):
        z = z + jnp.exp(tv[r] - tv[0])
    c1 = jnp.zeros((n, tb), I32)
    for r in range(topk):
        c1 = jnp.where(r1 == r, rowcount[r // sub][r % sub:r % sub + 1, :], c1)
    return jnp.exp(s1 - a[0]) / z, c1, jnp.exp(s2 - b[0]), r2, ok


def _peer_route_kernel(s_ref, e1_ref, c1_ref, e2_ref, r2_ref, *, topk):
    heads = e1_ref.shape[0]
    tb = s_ref.shape[2]
    bad = jnp.zeros((1, tb), I32)
    for h in range(heads):
        e1, c1, e2, r2, ok = _route_head_fast(s_ref[2 * h], s_ref[2 * h + 1], topk)
        e1_ref[h], c1_ref[h], e2_ref[h], r2_ref[h] = e1, c1, e2, r2
        bad = jnp.where(ok, bad, 1)

    @pl.when(jnp.max(bad) > 0)
    def _():
        def redo(h, carry):
            e1, c1, e2, r2 = _route_head_exact(s_ref[2 * h], s_ref[2 * h + 1], topk)
            e1_ref[h], c1_ref[h], e2_ref[h], r2_ref[h] = e1, c1, e2, r2
            return carry

        lax.fori_loop(0, heads, redo, 0)


def _peer_route(scores, heads):
    hp, nk, n = scores.shape
    tb = LANES
    spec = pl.BlockSpec((heads, nk, tb), lambda i: (0, 0, i))
    return pl.pallas_call(
        functools.partial(_peer_route_kernel, topk=PEER_TOPK),
        out_shape=[jax.ShapeDtypeStruct((heads, nk, n), F32), jax.ShapeDtypeStruct((heads, nk, n), I32),
                   jax.ShapeDtypeStruct((heads, nk, n), F32), jax.ShapeDtypeStruct((heads, nk, n), I32)],
        grid=(n // tb,),
        in_specs=[pl.BlockSpec((hp, nk, tb), lambda i: (0, 0, i))],
        out_specs=[spec, spec, spec, spec],
        compiler_params=_cparams("parallel"),
        name="peer_route",
    )(scores)


def _gelu_tanh(x):
    c = math.sqrt(2.0 / math.pi)
    return x * (0.5 * (1.0 + jnp.tanh(c * (x + 0.044715 * (x * x * x)))))


def _peer_dense_kernel(xt_ref, u_ref, vt_ref, e1_ref, c1_ref, e2_ref, r2_ref, y_ref, acc_ref, a_ref, *, ti, sub):
    j = pl.program_id(1)
    nj = pl.num_programs(1) - 1
    heads, nk, tb = e2_ref.shape

    @pl.when(j == 0)
    def _():
        acc_ref[...] = jnp.zeros_like(acc_ref)
        a_ref[1] = jnp.zeros(a_ref.shape[1:], a_ref.dtype)

    jc = jnp.minimum(j, nj - 1)
    tiles = [slice(s * sub * nk, (s + 1) * sub * nk) for s in range(ti // sub)]
    acts = [_dot(u_ref[rows, :], xt_ref[...]) for rows in tiles]
    acc_ref[...] += _dot(vt_ref[...], a_ref[(j + 1) % 2])
    for s, rows in enumerate(tiles):
        act = _gelu_tanh(acts[s])
        parts = []
        for t in range(sub):
            i1 = pl.ds(jc * ti + s * sub + t, 1)
            g = jnp.zeros((nk, tb), F32)
            for h in range(heads):
                hit = r2_ref[h] < c1_ref[h, i1, :]
                g = g + jnp.where(hit, e1_ref[h, i1, :] * e2_ref[h], 0.0)
            parts.append((act[t * nk:(t + 1) * nk, :] * g).astype(BF16))
        a_ref[j % 2, rows, :] = jnp.concatenate(parts, axis=0)

    @pl.when(j == nj)
    def _():
        y_ref[...] = acc_ref[...].T


def _peer_dense(x1t, u_b, vt_b, e1, c1, e2, r2):
    d, n = x1t.shape
    ne = u_b.shape[0]
    heads, nk, _ = e1.shape
    tb = _pick(n, 512, 256, 128)
    te = PEER_TE
    nj = ne // te
    ti = te // nk
    sub = 2
    once = pl.Buffered(1)
    rspec = pl.BlockSpec((heads, nk, tb), lambda i, j: (0, 0, i), pipeline_mode=once)
    return pl.pallas_call(
        functools.partial(_peer_dense_kernel, ti=ti, sub=sub),
        out_shape=jax.ShapeDtypeStruct((n, d), F32),
        grid=(n // tb, nj + 1),
        in_specs=[pl.BlockSpec((d, tb), lambda i, j: (0, i)),
                  pl.BlockSpec((te, d), lambda i, j: (jnp.minimum(j, nj - 1), 0)),
                  pl.BlockSpec((d, te), lambda i, j: (0, jnp.maximum(j - 1, 0))),
                  rspec, rspec, rspec, rspec],
        out_specs=pl.BlockSpec((tb, d), lambda i, j: (i, 0), pipeline_mode=once),
        scratch_shapes=[pltpu.VMEM((d, tb), F32), pltpu.VMEM((2, te, tb), BF16)],
        compiler_params=_cparams("parallel", "arbitrary"),
        name="peer_dense",
    )(x1t, u_b, vt_b, e1, c1, e2, r2)


def _cast_kernel(x_ref, o_ref):
    o_ref[...] = x_ref[...].astype(BF16)


def _cast_t_kernel(x_ref, o_ref):
    o_ref[...] = x_ref[...].T.astype(BF16)


def _cast_rows(x, transpose):
    r, d = x.shape
    tr = _pick(r, 512, 256, 128)
    if transpose:
        kern, oshape, ospec = _cast_t_kernel, (d, r), pl.BlockSpec((d, tr), lambda i: (0, i))
    else:
        kern, oshape, ospec = _cast_kernel, (r, d), pl.BlockSpec((tr, d), lambda i: (i, 0))
    return pl.pallas_call(
        kern,
        out_shape=jax.ShapeDtypeStruct(oshape, BF16),
        grid=(r // tr,),
        in_specs=[pl.BlockSpec((tr, d), lambda i: (i, 0))],
        out_specs=ospec,
        compiler_params=_cparams("parallel"),
        name="cast_t" if transpose else "cast",
    )(x)


def _final_kernel(xf_ref, xb_ref, ch_ref, p_ref, wg_ref, wp_ref, g_ref, b_ref, o_ref, acc0_ref, acc1_ref, *, tn,
                  nchunks):
    j = pl.program_id(1)

    def step(fill_ref, done_ref):
        cols = pl.ds(pl.multiple_of(j * tn, tn), tn)
        gate = _sigmoid(_dot(xb_ref[...], wg_ref[...]))
        proj = _dot(p_ref[...].astype(BF16), wp_ref[...])
        fill_ref[:, cols] = DEEPNORM_ALPHA * xf_ref[...] + ch_ref[...] + gate * proj

        def norm_chunk():
            rows = pl.ds(pl.multiple_of(j * LANES, LANES), LANES)
            o_ref[...] = _layer_norm(done_ref[rows, :], g_ref[...], b_ref[...])

        _lag_epilogue(j, nchunks, pl.num_programs(1), norm_chunk)

    _lag_run(acc0_ref, acc1_ref, step)


def _final(x1f, x1b, ch, p, wg_b, wp_b, g, b):
    n, d = x1f.shape
    pd = p.shape[1]
    tn = 1024
    ns = d // tn
    tm, mt, nchunks, cur, chunk = _lag_specs(n, ns)
    vec = pl.BlockSpec((1, d), lambda i, j: (0, 0))
    rows = lambda w: pl.BlockSpec((tm, w), lambda i, j: (cur(i), 0), pipeline_mode=pl.Buffered(1))
    tile = pl.BlockSpec((tm, tn), lambda i, j: (cur(i), j))
    return pl.pallas_call(
        functools.partial(_final_kernel, tn=tn, nchunks=nchunks),
        out_shape=jax.ShapeDtypeStruct((n, d), F32),
        grid=(mt + 1, ns),
        in_specs=[tile, rows(d), tile, rows(pd),
                  pl.BlockSpec((d, tn), lambda i, j: (0, j)),
                  pl.BlockSpec((pd, tn), lambda i, j: (0, j)),
                  vec, vec],
        out_specs=pl.BlockSpec((LANES, d), lambda i, j: (chunk(i, j), 0)),
        scratch_shapes=[pltpu.VMEM((tm, d), F32), pltpu.VMEM((tm, d), F32)],
        compiler_params=_cparams("arbitrary", "arbitrary"),
        name="ple_ln2",
    )(x1f, x1b, ch, p, wg_b, wp_b, g.reshape(1, d), b.reshape(1, d))


def _rope_tables(pos, half):
    inv = ROPE_BASE ** (-jnp.arange(half, dtype=F32) / half)
    ang = pos.astype(F32)[:, None] * inv[None]
    return jnp.cos(ang), jnp.sin(ang)


def _post_mixer(o_r, o_m, x_raw, p, wts):
    x1f, x1b, x1t = _out_proj_ln1(o_r, o_m, wts["w_out"], x_raw, wts["ln_emb_g"], wts["ln_emb_b"], wts["ln1_g"],
                                  wts["ln1_b"])
    heads = wts["keys"].shape[0] // 2
    scores = _peer_scores(x1b, wts["w_q"], wts["keys"])
    e1, rb, e2, b2 = _peer_route(scores, heads)
    ch = _peer_dense(x1t, wts["u"], wts["vt"], e1, rb, e2, b2)
    return _final(x1f, x1b, ch, p, wts["w_gate"], wts["w_proj"], wts["ln2_g"], wts["ln2_b"])


def kernel(x_prompt, x_sample, state_ret, state_conv, state_mlstm_c, state_mlstm_n, state_mlstm_m, p_prompt, p_sample,
           ln_emb_g, ln_emb_b, w_in, b_gate, conv_w, conv_b, g_ret_norm, g_ml_norm, w_out, ln1_g, ln1_b,
           w_peer_q, peer_sub_keys, peer_u, peer_v, w_ple_gate, w_ple_proj, ln2_g, ln2_b):
    bsz, t, d = x_prompt.shape
    nb = x_sample.shape[0]
    _, _, heads, dk, _ = state_ret.shape
    rw = heads * dk
    n_main = 8 * rw
    assert x_sample.shape[1] == 1 and w_in.shape[0] == DEPTH and t % CHUNK == 0

    w_in_t = jnp.swapaxes(w_in, 1, 2)
    wg_t = w_in_t[0, n_main:, :].astype(BF16)
    log_g = jnp.log1p(-(2.0 ** (-5.0 - jnp.arange(heads, dtype=F32))))
    keys = peer_sub_keys[0]
    wts = dict(
        ln_emb_g=ln_emb_g, ln_emb_b=ln_emb_b, ln1_g=ln1_g[0], ln1_b=ln1_b[0], ln2_g=ln2_g[0], ln2_b=ln2_b[0],
        w_out=w_out[0].astype(BF16), w_q=w_peer_q[0].astype(BF16),
        keys=keys.reshape(keys.shape[0] * 2, keys.shape[2], keys.shape[3]).astype(BF16),
        u=_cast_rows(peer_u[0], False), vt=_cast_rows(peer_v[0], True),
        w_gate=w_ple_gate[0].astype(BF16), w_proj=w_ple_proj[0].astype(BF16),
    )

    xp = x_prompt.reshape(bsz * t, d)
    xn = _ln_cast(xp, ln_emb_g, ln_emb_b)
    z = _in_proj(xn, w_in_t, n_main)
    gates_t = _gate_rows(xn, wg_t)
    cos, sin = _rope_tables(jnp.arange(t), dk // 2)
    o_r, ret_p = _ret_prompt(z, log_g, cos, sin, g_ret_norm[0], bsz, t, heads, dk, 0)
    o_m, c_p, n_p, m_p = _mlstm_prompt(z, gates_t, b_gate[0], conv_w[0], conv_b[0].reshape(1, 2 * rw), g_ml_norm[0],
                                       bsz, t, heads, dk, 4 * rw)
    conv_p = z.reshape(bsz, t, n_main)[:, t - (CONV_W - 1):, 4 * rw:6 * rw]
    y_prompt = _post_mixer(o_r, o_m, xp, p_prompt[0].reshape(bsz * t, -1), wts).reshape(bsz, t, d)

    xs = x_sample.reshape(nb, d)
    xns = _ln_cast(xs, ln_emb_g, ln_emb_b)
    zs = _in_proj(xns, w_in_t, n_main)
    gates_s = _gate_rows(xns, wg_t).T
    cs, sn = _rope_tables(jnp.full((1,), PAST_LEN), dk // 2)
    o_rs, o_ms, ret_s, conv_s, c_s, n_s, m_s = _mix_sample(
        zs, gates_s, log_g, b_gate[0], jnp.concatenate([cs, sn], axis=0), state_ret[0], state_conv[0],
        state_mlstm_c[0], state_mlstm_n[0], state_mlstm_m[0], conv_w[0], conv_b[0], g_ret_norm[0], g_ml_norm[0], heads, dk)
    y_sample = _post_mixer(o_rs, o_ms, xs, p_sample[0].reshape(nb, -1), wts).reshape(nb, 1, d)

    lead = lambda a: a[None]
    return (y_prompt, y_sample, lead(ret_p), lead(conv_p), lead(c_p), lead(n_p), lead(m_p),
            lead(ret_s), lead(conv_s), lead(c_s), lead(n_s), lead(m_s))
```
